```python
import math
import jax, jax.numpy as jnp
from jax import lax
import numpy as np

D_MODEL = 1024
BATCH = 4
SEQ = 4096
DEPTH = 4

D_MLSTM = D_MODEL
N_MLSTM_HEADS = 4
MLSTM_HEAD_DIM = D_MLSTM // N_MLSTM_HEADS
D_CONV = D_MODEL
N_CONV_GROUPS = 8
D_MIX = D_MLSTM + D_CONV
D_IN_PROJ = 2 * D_MLSTM + 2 * D_CONV
MLSTM_CONV_K = 4
CONFORMER_CONV_K = 31
CHUNK = 128
D_FF = 4 * D_MODEL
N_ADA = 6
EPS = 1e-6

kernel_name = "hybrid_mlstm_conformer_conv_sandwich_adaln"


def rms_norm(x, g):
    xf = x.astype(jnp.float32)
    y = xf * lax.rsqrt(jnp.mean(xf * xf, axis=-1, keepdims=True) + EPS)
    return (y * g.astype(jnp.float32)).astype(x.dtype)


def group_layer_norm(x, n_groups, g, b=None):
    shp = x.shape
    xf = x.astype(jnp.float32).reshape(*shp[:-1], n_groups, shp[-1] // n_groups)
    mu = jnp.mean(xf, axis=-1, keepdims=True)
    xc = xf - mu
    var = jnp.mean(xc * xc, axis=-1, keepdims=True)
    y = (xc * lax.rsqrt(var + EPS)).reshape(shp) * g.astype(jnp.float32)
    if b is not None:
        y = y + b.astype(jnp.float32)
    return y.astype(x.dtype)


def causal_depthwise_conv(x, w, b):
    K, C = w.shape
    y = lax.conv_general_dilated(
        x, w[:, None, :].astype(x.dtype), window_strides=(1,), padding=[(K - 1, 0)],
        dimension_numbers=("NWC", "WIO", "NWC"), feature_group_count=C)
    return y + b.astype(x.dtype)


def mlstm_chunkwise(q, k, v, ig, lf):
    B, T, H, Dh = q.shape
    nc = T // CHUNK

    def to_chunks(a):
        a = a.astype(jnp.float32).reshape(B, nc, CHUNK, H, *a.shape[3:])
        return jnp.moveaxis(a, (1, 3), (0, 2))

    xs = (to_chunks(q), to_chunks(k), to_chunks(v), to_chunks(ig), to_chunks(lf))
    causal = jnp.tril(jnp.ones((CHUNK, CHUNK), dtype=bool))

    def body(carry, inp):
        C, n, m = carry
        qc, kc, vc, igc, lfc = inp
        b = jnp.cumsum(lfc, axis=-1)
        logw = b[..., :, None] - b[..., None, :] + igc[..., None, :]
        logw = jnp.where(causal, logw, -jnp.inf)
        g = b + m[..., None]
        m_t = jnp.maximum(g, jnp.max(logw, axis=-1))
        w = jnp.exp(logw - m_t[..., None])
        inter = jnp.exp(g - m_t)
        s = jnp.einsum("bhtd,bhsd->bhts", qc, kc) * w
        num = jnp.einsum("bhts,bhse->bhte", s, vc) + inter[..., None] * jnp.einsum("bhtd,bhde->bhte", qc, C)
        den = jnp.sum(s, axis=-1) + inter * jnp.einsum("bhtd,bhd->bht", qc, n)
        h = num / jnp.maximum(jnp.abs(den), jnp.exp(-m_t))[..., None]
        b_last = b[..., -1]
        a = b_last[..., None] - b + igc
        m_new = jnp.maximum(b_last + m, jnp.max(a, axis=-1))
        wk = jnp.exp(a - m_new[..., None])
        decay = jnp.exp(b_last + m - m_new)
        C_new = decay[..., None, None] * C + jnp.einsum("bhs,bhsd,bhse->bhde", wk, kc, vc)
        n_new = decay[..., None] * n + jnp.einsum("bhs,bhsd->bhd", wk, kc)
        return (C_new, n_new, m_new), h

    init = (jnp.zeros((B, H, Dh, Dh), jnp.float32), jnp.zeros((B, H, Dh), jnp.float32),
            jnp.zeros((B, H), jnp.float32))
    _, hs = lax.scan(body, init, xs)
    return jnp.moveaxis(hs, (0, 2), (1, 3)).reshape(B, T, H, Dh)


def mixer(h, w_in, w_conv_m, b_conv_m, w_q, w_k, w_v, w_gates, b_gates, g_mh,
          w_dw, b_dw, g_cn, b_cn, w_out):
    B, T, _ = h.shape
    H, Dh = N_MLSTM_HEADS, MLSTM_HEAD_DIM
    u = h @ w_in.astype(h.dtype)
    x_m, z, glu_a, glu_b = jnp.split(u, [D_MLSTM, 2 * D_MLSTM, 2 * D_MLSTM + D_CONV], axis=-1)

    x_c = jax.nn.silu(causal_depthwise_conv(x_m, w_conv_m, b_conv_m))
    xc_h = x_c.reshape(B, T, H, Dh)
    xm_h = x_m.reshape(B, T, H, Dh)
    q = jnp.einsum("bthd,hde->bthe", xc_h, w_q.astype(h.dtype))
    k = jnp.einsum("bthd,hde->bthe", xc_h, w_k.astype(h.dtype))
    v = jnp.einsum("bthd,hde->bthe", xm_h, w_v.astype(h.dtype))
    qkv = jnp.concatenate([q, k, v], axis=-1).reshape(B, T, H, 3 * Dh)
    qkv = jnp.moveaxis(qkv, 2, 3).reshape(B, T, 3 * D_MLSTM)
    gates = (qkv @ w_gates.astype(h.dtype) + b_gates.astype(h.dtype)).astype(jnp.float32)
    ig, fg = gates[..., :H], gates[..., H:]
    lf = jax.nn.log_sigmoid(fg)
    h_m = mlstm_chunkwise(q, k * (Dh ** -0.5), v, ig, lf)
    h_m = group_layer_norm(h_m.reshape(B, T, D_MLSTM), H, g_mh).astype(h.dtype)
    h_m = jax.nn.sigmoid(z) * h_m

    a = glu_a * jax.nn.sigmoid(glu_b)
    a = causal_depthwise_conv(a, w_dw, b_dw)
    a = jax.nn.silu(group_layer_norm(a, N_CONV_GROUPS, g_cn, b_cn))

    return jnp.concatenate([h_m, a], axis=-1) @ w_out.astype(h.dtype)


def setup_inputs(seed: int = 0) -> dict:
    key = jax.random.key(seed)
    ks = jax.random.split(key, 26)
    f32 = jnp.float32

    def nrm(k, shape, fan_in, scale=1.0):
        return jax.random.normal(k, shape, f32) * (scale * fan_in ** -0.5)

    def gain(k, shape):
        return 1.0 + 0.05 * jax.random.normal(k, shape, f32)

    def small(k, shape):
        return 0.02 * jax.random.normal(k, shape, f32)

    H = N_MLSTM_HEADS
    ig_bias = 0.1 * jax.random.normal(ks[24], (DEPTH, H), f32)
    fg_bias = jnp.linspace(3.0, 6.0, H, dtype=f32)[None, :] + 0.1 * jax.random.normal(ks[25], (DEPTH, H), f32)
    return {
        "x": jax.random.normal(ks[0], (BATCH, SEQ, D_MODEL), f32),
        "c": jax.random.normal(ks[1], (BATCH, D_MODEL), f32),
        "w_ada": nrm(ks[2], (DEPTH, D_MODEL, N_ADA * D_MODEL), D_MODEL, 0.5),
        "b_ada": small(ks[3], (DEPTH, N_ADA * D_MODEL)),
        "g_pre_mix": gain(ks[4], (DEPTH, D_MODEL)),
        "g_post_mix": gain(ks[5], (DEPTH, D_MODEL)),
        "g_pre_mlp": gain(ks[6], (DEPTH, D_MODEL)),
        "g_post_mlp": gain(ks[7], (DEPTH, D_MODEL)),
        "w_in": nrm(ks[8], (DEPTH, D_MODEL, D_IN_PROJ), D_MODEL),
        "w_conv_m": nrm(ks[9], (DEPTH, MLSTM_CONV_K, D_MLSTM), MLSTM_CONV_K),
        "b_conv_m": small(ks[10], (DEPTH, D_MLSTM)),
        "w_q": nrm(ks[11], (DEPTH, H, MLSTM_HEAD_DIM, MLSTM_HEAD_DIM), MLSTM_HEAD_DIM),
        "w_k": nrm(ks[12], (DEPTH, H, MLSTM_HEAD_DIM, MLSTM_HEAD_DIM), MLSTM_HEAD_DIM),
        "w_v": nrm(ks[13], (DEPTH, H, MLSTM_HEAD_DIM, MLSTM_HEAD_DIM), MLSTM_HEAD_DIM),
        "w_gates": nrm(ks[14], (DEPTH, 3 * D_MLSTM, 2 * H), 3 * D_MLSTM),
        "b_gates": jnp.concatenate([ig_bias, fg_bias], axis=-1),
        "g_mh": gain(ks[15], (DEPTH, D_MLSTM)),
        "w_dw": nrm(ks[16], (DEPTH, CONFORMER_CONV_K, D_CONV), CONFORMER_CONV_K),
        "b_dw": small(ks[17], (DEPTH, D_CONV)),
        "g_cn": gain(ks[18], (DEPTH, D_CONV)),
        "b_cn": small(ks[19], (DEPTH, D_CONV)),
        "w_out": nrm(ks[20], (DEPTH, D_MIX, D_MODEL), D_MIX),
        "w_ff1": nrm(ks[21], (DEPTH, D_MODEL, D_FF), D_MODEL),
        "w_ff2": nrm(ks[22], (DEPTH, D_FF, D_MODEL), D_FF),
    }


def reference(x, c, w_ada, b_ada, g_pre_mix, g_post_mix, g_pre_mlp, g_post_mlp, w_in,
              w_conv_m, b_conv_m, w_q, w_k, w_v, w_gates, b_gates, g_mh, w_dw, b_dw,
              g_cn, b_cn, w_out, w_ff1, w_ff2):
    c_act = jax.nn.silu(c.astype(jnp.float32))
    for l in range(DEPTH):
        mod = (c_act @ w_ada[l].astype(jnp.float32) + b_ada[l].astype(jnp.float32)).astype(x.dtype)
        sh_mix, sc_mix, gt_mix, sh_mlp, sc_mlp, gt_mlp = [m[:, None, :] for m in jnp.split(mod, N_ADA, axis=-1)]

        h = rms_norm(x, g_pre_mix[l]) * (1 + sc_mix) + sh_mix
        y = mixer(h, w_in[l], w_conv_m[l], b_conv_m[l], w_q[l], w_k[l], w_v[l], w_gates[l], b_gates[l],
                  g_mh[l], w_dw[l], b_dw[l], g_cn[l], b_cn[l], w_out[l])
        x = x + gt_mix * rms_norm(y, g_post_mix[l])

        h = rms_norm(x, g_pre_mlp[l]) * (1 + sc_mlp) + sh_mlp
        f = jnp.square(jax.nn.relu(h @ w_ff1[l].astype(h.dtype))) @ w_ff2[l].astype(h.dtype)
        x = x + gt_mlp * rms_norm(f, g_post_mlp[l])
    return x
```

```python
import functools

import jax
import jax.numpy as jnp
from jax import lax
from jax.experimental import pallas as pl
from jax.experimental.pallas import tpu as pltpu

D_MODEL = 1024
N_HEADS = 4
HEAD_DIM = 256
N_CONV_GROUPS = 8
CONV_GROUP = D_MODEL // N_CONV_GROUPS
K_M = 4
K_C = 31
CHUNK = 128
D_FF = 4 * D_MODEL
N_ADA = 6
EPS = 1e-6

SUBLANES = 8
LANES = 128
VMEM_LIMIT_BYTES = 56 * 1024 * 1024

TT = 256
TM = 512
HIST_M = SUBLANES
HIST_C = 4 * SUBLANES
CONV_ROWS = 64

BF16 = jnp.bfloat16
F32 = jnp.float32


def _sigmoid(x):
    return 1.0 / (1.0 + jnp.exp(-x))


def _log_sigmoid(x):
    return jnp.minimum(x, 0.0) - jnp.log1p(jnp.exp(-jnp.abs(x)))


def _rms_scale(x):
    return lax.rsqrt(jnp.mean(x * x, axis=-1, keepdims=True) + EPS)


def _dot(a, b):
    return jnp.dot(a, b, preferred_element_type=F32)


def _ada_kernel(c_ref, w_ref, b_ref, o_ref):
    c = c_ref[...]
    c_act = c * _sigmoid(c)
    o_ref[0, 0] = _dot(c_act, w_ref[0]) + b_ref[0, 0]


def _ada_call(c, w_ada, b_ada):
    depth = w_ada.shape[0]
    batch = c.shape[0]
    return pl.pallas_call(
        _ada_kernel,
        out_shape=jax.ShapeDtypeStruct((depth, N_ADA, batch, D_MODEL), F32),
        grid=(depth, N_ADA),
        in_specs=[
            pl.BlockSpec((batch, D_MODEL), lambda l, j: (0, 0)),
            pl.BlockSpec((1, D_MODEL, D_MODEL), lambda l, j: (l, 0, j)),
            pl.BlockSpec((1, 1, 1, D_MODEL), lambda l, j: (l, j, 0, 0)),
        ],
        out_specs=pl.BlockSpec((1, 1, batch, D_MODEL), lambda l, j: (l, j, 0, 0)),
        compiler_params=pltpu.CompilerParams(
            dimension_semantics=("arbitrary", "arbitrary"),
            vmem_limit_bytes=VMEM_LIMIT_BYTES),
        name="ada_mod",
    )(c, w_ada, b_ada.reshape(depth, N_ADA, 1, D_MODEL))


def _mixer_kernel(x_ref, mod_ref, gpre_ref, gpost_ref, win_ref, wcm_ref, bcm_ref,
                  wq_ref, wk_ref, wv_ref, wg_ref, bg_ref, gmh_ref, wdw_ref, bdw_ref,
                  gcn_ref, bcn_ref, wout_ref, o_ref,
                  xm_ref, a_ref, q_ref, k_ref, v_ref, z_ref, hm_ref, ac_ref,
                  c_state, n_state, m_state):
    t_idx = pl.program_id(1)

    @pl.when(t_idx == 0)
    def _():
        xm_ref[0:HIST_M, :] = jnp.zeros((HIST_M, D_MODEL), F32)
        a_ref[0:HIST_C, :] = jnp.zeros((HIST_C, D_MODEL), F32)
        c_state[...] = jnp.zeros(c_state.shape, F32)
        n_state[...] = jnp.zeros(n_state.shape, F32)
        m_state[...] = jnp.zeros(m_state.shape, F32)

    x = x_ref[0]
    sh = mod_ref[0]
    sc = mod_ref[1]
    gt = mod_ref[2]

    h = x * _rms_scale(x) * (gpre_ref[...] * (1.0 + sc)) + sh
    hb = h.astype(BF16)

    xm_ref[HIST_M:HIST_M + TT, :] = _dot(hb, win_ref[:, 0:D_MODEL])
    z_ref[...] = _dot(hb, win_ref[:, D_MODEL:2 * D_MODEL])
    glu_a = _dot(hb, win_ref[:, 2 * D_MODEL:3 * D_MODEL])
    glu_b = _dot(hb, win_ref[:, 3 * D_MODEL:4 * D_MODEL])
    a_ref[HIST_C:HIST_C + TT, :] = glu_a * _sigmoid(glu_b)

    xc = bcm_ref[...] + wcm_ref[K_M - 1:K_M, :] * xm_ref[HIST_M:HIST_M + TT, :]
    for j in range(K_M - 1):
        off = HIST_M - (K_M - 1) + j
        xc = xc + wcm_ref[j:j + 1, :] * xm_ref[off:off + TT, :]
    xc = xc * _sigmoid(xc)
    xcb = xc.astype(BF16)
    xmb = xm_ref[HIST_M:HIST_M + TT, :].astype(BF16)
    for hd in range(N_HEADS):
        sl = slice(hd * HEAD_DIM, (hd + 1) * HEAD_DIM)
        q_ref[:, sl] = _dot(xcb[:, sl], wq_ref[hd])
        k_ref[:, sl] = _dot(xcb[:, sl], wk_ref[hd])
        v_ref[:, sl] = _dot(xmb[:, sl], wv_ref[hd])

    gates = (_dot(q_ref[...].astype(BF16), wg_ref[0])
             + _dot(k_ref[...].astype(BF16), wg_ref[1])
             + _dot(v_ref[...].astype(BF16), wg_ref[2]) + bg_ref[...])
    gates_t = gates.T
    g8 = gates_t[0:SUBLANES, :]
    lf8 = _log_sigmoid(g8)

    lane = lax.broadcasted_iota(jnp.int32, (SUBLANES, TT), 1)
    lane_in_chunk = lane % CHUNK
    cs8 = lf8
    shift = 1
    while shift < CHUNK:
        rolled = pltpu.roll(cs8, shift, axis=1)
        cs8 = cs8 + jnp.where(lane_in_chunk >= shift, rolled, 0.0)
        shift *= 2

    row_i = lax.broadcasted_iota(jnp.int32, (CHUNK, CHUNK), 0)
    col_i = lax.broadcasted_iota(jnp.int32, (CHUNK, CHUNK), 1)
    causal = col_i <= row_i
    diag = col_i == row_i

    for ck in range(TT // CHUNK):
        rows = slice(ck * CHUNK, (ck + 1) * CHUNK)
        for hd in range(N_HEADS):
            sl = slice(hd * HEAD_DIM, (hd + 1) * HEAD_DIM)
            ig_row = g8[hd:hd + 1, rows]
            lf_row = lf8[N_HEADS + hd:N_HEADS + hd + 1, rows]
            b_row = cs8[N_HEADS + hd:N_HEADS + hd + 1, rows]
            b_col = jnp.sum(jnp.where(causal, lf_row, 0.0), axis=1, keepdims=True)
            ig_col = jnp.sum(jnp.where(diag, ig_row, 0.0), axis=1, keepdims=True)
            b_last = jnp.sum(lf_row, axis=1, keepdims=True)
            m_prev = m_state[hd, 0:1, 0:1]

            logw = jnp.where(causal, b_col + (ig_row - b_row), -jnp.inf)
            g_col = b_col + m_prev
            m_t = jnp.maximum(g_col, jnp.max(logw, axis=1, keepdims=True))
            w = jnp.exp(logw - m_t)
            inter = jnp.exp(g_col - m_t)

            qf = q_ref[rows, sl]
            kf = k_ref[rows, sl] * (HEAD_DIM ** -0.5)
            vb = v_ref[rows, sl].astype(BF16)
            qb = qf.astype(BF16)
            s = lax.dot_general(qb, kf.astype(BF16), (((1,), (1,)), ((), ())),
                                preferred_element_type=F32) * w
            c_prev = c_state[hd]
            n_prev = n_state[hd]
            num = _dot(s.astype(BF16), vb) + inter * _dot(qb, c_prev.astype(BF16))
            den = (jnp.sum(s, axis=1, keepdims=True)
                   + inter * jnp.sum(qf * n_prev, axis=1, keepdims=True))
            hm_ref[rows, sl] = num / jnp.maximum(jnp.abs(den), jnp.exp(-m_t))

            a_col = b_last - b_col + ig_col
            m_new = jnp.maximum(b_last + m_prev, jnp.max(a_col, axis=0, keepdims=True))
            wk = jnp.exp(a_col - m_new)
            decay = jnp.exp(b_last + m_prev - m_new)
            kw = kf * wk
            c_state[hd] = decay * c_prev + lax.dot_general(
                kw.astype(BF16), vb, (((0,), (0,)), ((), ())), preferred_element_type=F32)
            n_state[hd] = decay * n_prev + jnp.sum(kw, axis=0, keepdims=True)
            m_state[hd] = jnp.broadcast_to(m_new, (SUBLANES, LANES))

    for hd in range(N_HEADS):
        sl = slice(hd * HEAD_DIM, (hd + 1) * HEAD_DIM)
        hh = hm_ref[:, sl]
        mu = jnp.mean(hh, axis=-1, keepdims=True)
        hc = hh - mu
        var = jnp.mean(hc * hc, axis=-1, keepdims=True)
        hn = hc * lax.rsqrt(var + EPS) * gmh_ref[:, sl]
        hm_ref[:, sl] = _sigmoid(z_ref[:, sl]) * hn

    def conv_body(lt, carry):
        c0 = pl.multiple_of(lt * LANES, LANES)
        cols = pl.ds(c0, LANES)
        for rb in range(TT // CONV_ROWS):
            r0 = rb * CONV_ROWS
            acc = jnp.broadcast_to(bdw_ref[:, cols], (CONV_ROWS, LANES))
            for j in range(K_C):
                off = r0 + HIST_C - (K_C - 1) + j
                acc = acc + wdw_ref[j:j + 1, cols] * a_ref[off:off + CONV_ROWS, cols]
            mu = jnp.mean(acc, axis=-1, keepdims=True)
            ctr = acc - mu
            var = jnp.mean(ctr * ctr, axis=-1, keepdims=True)
            y = ctr * lax.rsqrt(var + EPS) * gcn_ref[:, cols] + bcn_ref[:, cols]
            ac_ref[r0:r0 + CONV_ROWS, cols] = y * _sigmoid(y)
        return carry

    lax.fori_loop(0, N_CONV_GROUPS, conv_body, 0)

    y = (_dot(hm_ref[...].astype(BF16), wout_ref[0:D_MODEL, :])
         + _dot(ac_ref[...].astype(BF16), wout_ref[D_MODEL:2 * D_MODEL, :]))
    o_ref[0] = x + gt * (y * _rms_scale(y) * gpost_ref[...])

    xm_ref[HIST_M - (K_M - 1):HIST_M, :] = xm_ref[HIST_M + TT - (K_M - 1):HIST_M + TT, :]
    a_ref[HIST_C - (K_C - 1):HIST_C, :] = a_ref[HIST_C + TT - (K_C - 1):HIST_C + TT, :]


def _mixer_call(layer, x, mod, p):
    batch, seq, _ = x.shape
    n_t = seq // TT

    def row(name):
        return pl.BlockSpec((None, 1, D_MODEL), lambda b, t: (layer, 0, 0)), p[name]

    def full(name):
        arr = p[name]
        blk = (None,) + arr.shape[1:]
        nd = arr.ndim - 1
        return pl.BlockSpec(blk, lambda b, t: (layer,) + (0,) * nd), arr

    specs_args = [
        (pl.BlockSpec((1, TT, D_MODEL), lambda b, t: (b, t, 0)), x),
        (pl.BlockSpec((None, 3, None, 1, D_MODEL), lambda b, t: (layer, 0, b, 0, 0)), mod),
        row("g_pre_mix"), row("g_post_mix"), full("w_in"), full("w_conv_m"), row("b_conv_m"),
        full("w_q"), full("w_k"), full("w_v"), full("w_gates"),
        (pl.BlockSpec((None, 1, LANES), lambda b, t: (layer, 0, 0)), p["b_gates"]),
        row("g_mh"), full("w_dw"), row("b_dw"), row("g_cn"), row("b_cn"), full("w_out"),
    ]
    in_specs = [s for s, _ in specs_args]
    args = [a for _, a in specs_args]
    return pl.pallas_call(
        _mixer_kernel,
        out_shape=jax.ShapeDtypeStruct(x.shape, F32),
        grid=(batch, n_t),
        in_specs=in_specs,
        out_specs=pl.BlockSpec((1, TT, D_MODEL), lambda b, t: (b, t, 0)),
        scratch_shapes=[
            pltpu.VMEM((HIST_M + TT, D_MODEL), F32),
            pltpu.VMEM((HIST_C + TT, D_MODEL), F32),
            pltpu.VMEM((TT, D_MODEL), F32),
            pltpu.VMEM((TT, D_MODEL), F32),
            pltpu.VMEM((TT, D_MODEL), F32),
            pltpu.VMEM((TT, D_MODEL), F32),
            pltpu.VMEM((TT, D_MODEL), F32),
            pltpu.VMEM((TT, D_MODEL), F32),
            pltpu.VMEM((N_HEADS, HEAD_DIM, HEAD_DIM), F32),
            pltpu.VMEM((N_HEADS, 1, HEAD_DIM), F32),
            pltpu.VMEM((N_HEADS, SUBLANES, LANES), F32),
        ],
        compiler_params=pltpu.CompilerParams(
            dimension_semantics=("arbitrary", "arbitrary"),
            vmem_limit_bytes=VMEM_LIMIT_BYTES),
        name="mixer",
    )(*args)


def _mlp_kernel(x_ref, mod_ref, gpre_ref, gpost_ref, w1_ref, w2_ref, o_ref):
    x = x_ref[0]
    sh = mod_ref[0]
    sc = mod_ref[1]
    gt = mod_ref[2]
    h = x * _rms_scale(x) * (gpre_ref[...] * (1.0 + sc)) + sh
    hb = h.astype(BF16)
    f = jnp.zeros((TM, D_MODEL), F32)
    for c in range(D_FF // D_MODEL):
        cs = slice(c * D_MODEL, (c + 1) * D_MODEL)
        u = jnp.maximum(_dot(hb, w1_ref[:, cs]), 0.0)
        f = f + _dot((u * u).astype(BF16), w2_ref[cs, :])
    o_ref[0] = x + gt * (f * _rms_scale(f) * gpost_ref[...])


def _mlp_call(layer, x, mod, p):
    batch, seq, _ = x.shape
    n_t = seq // TM
    row = lambda b, t: (layer, 0, 0)
    return pl.pallas_call(
        _mlp_kernel,
        out_shape=jax.ShapeDtypeStruct(x.shape, F32),
        grid=(batch, n_t),
        in_specs=[
            pl.BlockSpec((1, TM, D_MODEL), lambda b, t: (b, t, 0)),
            pl.BlockSpec((None, 3, None, 1, D_MODEL), lambda b, t: (layer, 0, b, 0, 0)),
            pl.BlockSpec((None, 1, D_MODEL), row),
            pl.BlockSpec((None, 1, D_MODEL), row),
            pl.BlockSpec((None, D_MODEL, D_FF), row),
            pl.BlockSpec((None, D_FF, D_MODEL), row),
        ],
        out_specs=pl.BlockSpec((1, TM, D_MODEL), lambda b, t: (b, t, 0)),
        compiler_params=pltpu.CompilerParams(
            dimension_semantics=("arbitrary", "arbitrary"),
            vmem_limit_bytes=VMEM_LIMIT_BYTES),
        name="mlp",
    )(x, mod, p["g_pre_mlp"], p["g_post_mlp"], p["w_ff1"], p["w_ff2"])


def kernel(x, c, w_ada, b_ada, g_pre_mix, g_post_mix, g_pre_mlp, g_post_mlp, w_in, w_conv_m, b_conv_m, w_q, w_k, w_v, w_gates, b_gates, g_mh, w_dw, b_dw, g_cn, b_cn, w_out, w_ff1, w_ff2):
    depth = w_in.shape[0]
    batch = x.shape[0]
    assert x.shape[1] % TM == 0 and x.shape[1] % TT == 0 and TT % CHUNK == 0

    def rows(a):
        return a.reshape(depth, 1, a.shape[-1]).astype(F32)

    wg = w_gates.reshape(depth, 3, HEAD_DIM, N_HEADS, 2 * N_HEADS)
    wg = wg.transpose(0, 1, 3, 2, 4).reshape(depth, 3, D_MODEL, 2 * N_HEADS)
    wg = jnp.pad(wg, ((0, 0), (0, 0), (0, 0), (0, LANES - 2 * N_HEADS))).astype(BF16)
    bg = jnp.pad(b_gates.astype(F32), ((0, 0), (0, LANES - 2 * N_HEADS))).reshape(depth, 1, LANES)

    p = {
        "g_pre_mix": rows(g_pre_mix), "g_post_mix": rows(g_post_mix),
        "g_pre_mlp": rows(g_pre_mlp), "g_post_mlp": rows(g_post_mlp),
        "w_in": w_in.astype(BF16), "w_conv_m": w_conv_m.astype(F32), "b_conv_m": rows(b_conv_m),
        "w_q": w_q.astype(BF16), "w_k": w_k.astype(BF16), "w_v": w_v.astype(BF16),
        "w_gates": wg, "b_gates": bg, "g_mh": rows(g_mh),
        "w_dw": w_dw.astype(F32), "b_dw": rows(b_dw), "g_cn": rows(g_cn), "b_cn": rows(b_cn),
        "w_out": w_out.astype(BF16), "w_ff1": w_ff1.astype(BF16), "w_ff2": w_ff2.astype(BF16),
    }

    mod = _ada_call(c.astype(F32), w_ada.astype(F32), b_ada.astype(F32))
    mod = mod.reshape(depth, 2, 3, batch, 1, D_MODEL)

    for layer in range(depth):
        x = _mixer_call(layer, x, mod[:, 0], p)
        x = _mlp_call(layer, x, mod[:, 1], p)
    return x
```

```python
import functools

import jax
import jax.numpy as jnp
from jax import lax
from jax.experimental import pallas as pl
from jax.experimental.pallas import tpu as pltpu

D_MODEL = 1024
N_HEADS = 4
HEAD_DIM = 256
N_CONV_GROUPS = 8
CONV_GROUP = D_MODEL // N_CONV_GROUPS
K_M = 4
K_C = 31
CHUNK = 128
D_FF = 4 * D_MODEL
N_ADA = 6
EPS = 1e-6

SUBLANES = 8
LANES = 128
VMEM_LIMIT_BYTES = 56 * 1024 * 1024

TT = 256
TM = 512
HIST_M = SUBLANES
HIST_C = 4 * SUBLANES
CONV_ROWS = 64
N_LANE_TILES = D_MODEL // LANES
assert CONV_GROUP == LANES

BF16 = jnp.bfloat16
F32 = jnp.float32


def _sigmoid(x):
    return jax.nn.sigmoid(x)


def _log_sigmoid(x):
    return jnp.minimum(x, 0.0) - jnp.log1p(jnp.exp(-jnp.abs(x)))


def _rms_scale(x):
    return lax.rsqrt(jnp.mean(x * x, axis=-1, keepdims=True) + EPS)


def _dot(a, b):
    return jnp.dot(a, b, preferred_element_type=F32)


def _ada_kernel(c_ref, w_ref, b_ref, o_ref):
    c = c_ref[...]
    c_act = c * _sigmoid(c)
    o_ref[0, 0] = _dot(c_act, w_ref[0]) + b_ref[0, 0]


def _ada_call(c, w_ada, b_ada):
    depth = w_ada.shape[0]
    batch = c.shape[0]
    return pl.pallas_call(
        _ada_kernel,
        out_shape=jax.ShapeDtypeStruct((depth, N_ADA, batch, D_MODEL), F32),
        grid=(depth, N_ADA),
        in_specs=[
            pl.BlockSpec((batch, D_MODEL), lambda l, j: (0, 0)),
            pl.BlockSpec((1, D_MODEL, D_MODEL), lambda l, j: (l, 0, j)),
            pl.BlockSpec((1, 1, 1, D_MODEL), lambda l, j: (l, j, 0, 0)),
        ],
        out_specs=pl.BlockSpec((1, 1, batch, D_MODEL), lambda l, j: (l, j, 0, 0)),
        compiler_params=pltpu.CompilerParams(
            dimension_semantics=("arbitrary", "arbitrary"),
            vmem_limit_bytes=VMEM_LIMIT_BYTES),
        name="ada_mod",
    )(c, w_ada, b_ada.reshape(depth, N_ADA, 1, D_MODEL))


def _causal_dwconv(src_ref, hist, w_ref, b_ref, k, lt, post, dst_ref):
    cols = slice(lt * LANES, (lt + 1) * LANES)
    for span in range(TT // (2 * CONV_ROWS)):
        for phase in range(2):
            r0 = span * 2 * CONV_ROWS + phase
            acc = jnp.broadcast_to(b_ref[:, cols], (CONV_ROWS, LANES))
            for j in range(k):
                off = r0 + hist - (k - 1) + j
                acc = acc + w_ref[j:j + 1, cols] * src_ref[lt, pl.ds(off, CONV_ROWS, stride=2), :]
            dst_ref[lt, pl.ds(r0, CONV_ROWS, stride=2), :] = post(acc, cols)


def _mixer_kernel(x_ref, mod_ref, gpre_ref, gpost_ref, win_ref, wcm_ref, bcm_ref,
                  wq_ref, wk_ref, wv_ref, wg_ref, bg_ref, gmh_ref, wdw_ref, bdw_ref,
                  gcn_ref, bcn_ref, wout_ref, o_ref,
                  xm_ref, a_ref, xc_ref, q_ref, k_ref, v_ref, z_ref, hm_ref, ac_ref,
                  c_state, n_state, m_state):
    t_idx = pl.program_id(1)

    @pl.when(t_idx == 0)
    def _():
        xm_ref[:, 0:HIST_M, :] = jnp.zeros((N_LANE_TILES, HIST_M, LANES), F32)
        a_ref[:, 0:HIST_C, :] = jnp.zeros((N_LANE_TILES, HIST_C, LANES), F32)
        c_state[...] = jnp.zeros(c_state.shape, F32)
        n_state[...] = jnp.zeros(n_state.shape, F32)
        m_state[...] = jnp.zeros(m_state.shape, F32)

    x = x_ref[0]
    sh = mod_ref[0]
    sc = mod_ref[1]
    gt = mod_ref[2]

    h = x * _rms_scale(x) * (gpre_ref[...] * (1.0 + sc)) + sh
    hb = h.astype(BF16)

    x_m = _dot(hb, win_ref[:, 0:D_MODEL])
    z_ref[...] = _dot(hb, win_ref[:, D_MODEL:2 * D_MODEL])
    glu_a = _dot(hb, win_ref[:, 2 * D_MODEL:3 * D_MODEL])
    glu_b = _dot(hb, win_ref[:, 3 * D_MODEL:4 * D_MODEL])
    a_val = glu_a * _sigmoid(glu_b)
    for lt in range(N_LANE_TILES):
        cols = slice(lt * LANES, (lt + 1) * LANES)
        xm_ref[lt, HIST_M:HIST_M + TT, :] = x_m[:, cols]
        a_ref[lt, HIST_C:HIST_C + TT, :] = a_val[:, cols]

    def swish(acc, cols):
        return acc * _sigmoid(acc)

    for lt in range(N_LANE_TILES):
        _causal_dwconv(xm_ref, HIST_M, wcm_ref, bcm_ref, K_M, lt, swish, xc_ref)
    xmb = x_m.astype(BF16)
    for hd in range(N_HEADS):
        sl = slice(hd * HEAD_DIM, (hd + 1) * HEAD_DIM)
        xcb = jnp.concatenate([xc_ref[2 * hd], xc_ref[2 * hd + 1]], axis=1).astype(BF16)
        q_ref[:, sl] = _dot(xcb, wq_ref[hd])
        k_ref[:, sl] = _dot(xcb, wk_ref[hd])
        v_ref[:, sl] = _dot(xmb[:, sl], wv_ref[hd])

    gates = (_dot(q_ref[...].astype(BF16), wg_ref[0])
             + _dot(k_ref[...].astype(BF16), wg_ref[1])
             + _dot(v_ref[...].astype(BF16), wg_ref[2]) + bg_ref[...])
    gates_t = gates.T
    g8 = gates_t[0:SUBLANES, :]
    lf8 = _log_sigmoid(g8)

    lane = lax.broadcasted_iota(jnp.int32, (SUBLANES, TT), 1)
    lane_in_chunk = lane % CHUNK
    cs8 = lf8
    shift = 1
    while shift < CHUNK:
        rolled = pltpu.roll(cs8, shift, axis=1)
        cs8 = cs8 + jnp.where(lane_in_chunk >= shift, rolled, 0.0)
        shift *= 2

    row_i = lax.broadcasted_iota(jnp.int32, (CHUNK, CHUNK), 0)
    col_i = lax.broadcasted_iota(jnp.int32, (CHUNK, CHUNK), 1)
    causal = col_i <= row_i
    diag = col_i == row_i

    for ck in range(TT // CHUNK):
        rows = slice(ck * CHUNK, (ck + 1) * CHUNK)
        for hd in range(N_HEADS):
            sl = slice(hd * HEAD_DIM, (hd + 1) * HEAD_DIM)
            ig_row = g8[hd:hd + 1, rows]
            lf_row = lf8[N_HEADS + hd:N_HEADS + hd + 1, rows]
            b_row = cs8[N_HEADS + hd:N_HEADS + hd + 1, rows]
            b_col = jnp.sum(jnp.where(causal, lf_row, 0.0), axis=1, keepdims=True)
            ig_col = jnp.sum(jnp.where(diag, ig_row, 0.0), axis=1, keepdims=True)
            b_last = jnp.sum(lf_row, axis=1, keepdims=True)
            m_prev = m_state[hd, 0:1, 0:1]

            logw = jnp.where(causal, b_col + (ig_row - b_row), -jnp.inf)
            g_col = b_col + m_prev
            m_t = jnp.maximum(g_col, jnp.max(logw, axis=1, keepdims=True))
            w = jnp.exp(logw - m_t)
            inter = jnp.exp(g_col - m_t)

            qf = q_ref[rows, sl]
            kf = k_ref[rows, sl] * (HEAD_DIM ** -0.5)
            vb = v_ref[rows, sl].astype(BF16)
            qb = qf.astype(BF16)
            s = lax.dot_general(qb, kf.astype(BF16), (((1,), (1,)), ((), ())),
                                preferred_element_type=F32) * w
            c_prev = c_state[hd]
            n_prev = n_state[hd]
            num = _dot(s.astype(BF16), vb) + inter * _dot(qb, c_prev.astype(BF16))
            den = (jnp.sum(s, axis=1, keepdims=True)
                   + inter * jnp.sum(qf * n_prev, axis=1, keepdims=True))
            hm_ref[rows, sl] = num / jnp.maximum(jnp.abs(den), jnp.exp(-m_t))

            a_col = b_last - b_col + ig_col
            m_new = jnp.maximum(b_last + m_prev, jnp.max(a_col, axis=0, keepdims=True))
            wk = jnp.exp(a_col - m_new)
            decay = jnp.exp(b_last + m_prev - m_new)
            kw = kf * wk
            c_state[hd] = decay * c_prev + lax.dot_general(
                kw.astype(BF16), vb, (((0,), (0,)), ((), ())), preferred_element_type=F32)
            n_state[hd] = decay * n_prev + jnp.sum(kw, axis=0, keepdims=True)
            m_state[hd] = jnp.broadcast_to(m_new, (SUBLANES, LANES))

    for hd in range(N_HEADS):
        sl = slice(hd * HEAD_DIM, (hd + 1) * HEAD_DIM)
        hh = hm_ref[:, sl]
        mu = jnp.mean(hh, axis=-1, keepdims=True)
        hc = hh - mu
        var = jnp.mean(hc * hc, axis=-1, keepdims=True)
        hn = hc * lax.rsqrt(var + EPS) * gmh_ref[:, sl]
        hm_ref[:, sl] = _sigmoid(z_ref[:, sl]) * hn

    def ln_swish(acc, cols):
        mu = jnp.mean(acc, axis=-1, keepdims=True)
        ctr = acc - mu
        var = jnp.mean(ctr * ctr, axis=-1, keepdims=True)
        y = ctr * lax.rsqrt(var + EPS) * gcn_ref[:, cols] + bcn_ref[:, cols]
        return y * _sigmoid(y)

    for lt in range(N_LANE_TILES):
        _causal_dwconv(a_ref, HIST_C, wdw_ref, bdw_ref, K_C, lt, ln_swish, ac_ref)

    ac = jnp.concatenate([ac_ref[lt] for lt in range(N_LANE_TILES)], axis=1)
    y = (_dot(hm_ref[...].astype(BF16), wout_ref[0:D_MODEL, :])
         + _dot(ac.astype(BF16), wout_ref[D_MODEL:2 * D_MODEL, :]))
    o_ref[0] = x + gt * (y * _rms_scale(y) * gpost_ref[...])

    xm_ref[:, 0:HIST_M, :] = xm_ref[:, TT:HIST_M + TT, :]
    a_ref[:, 0:HIST_C, :] = a_ref[:, TT:HIST_C + TT, :]


def _mixer_call(layer, x, mod, p):
    batch, seq, _ = x.shape
    n_t = seq // TT

    def row(name):
        return pl.BlockSpec((None, 1, D_MODEL), lambda b, t: (layer, 0, 0)), p[name]

    def full(name):
        arr = p[name]
        blk = (None,) + arr.shape[1:]
        nd = arr.ndim - 1
        return pl.BlockSpec(blk, lambda b, t: (layer,) + (0,) * nd), arr

    specs_args = [
        (pl.BlockSpec((1, TT, D_MODEL), lambda b, t: (b, t, 0)), x),
        (pl.BlockSpec((None, 3, None, 1, D_MODEL), lambda b, t: (layer, 0, b, 0, 0)), mod),
        row("g_pre_mix"), row("g_post_mix"), full("w_in"), full("w_conv_m"), row("b_conv_m"),
        full("w_q"), full("w_k"), full("w_v"), full("w_gates"),
        (pl.BlockSpec((None, 1, LANES), lambda b, t: (layer, 0, 0)), p["b_gates"]),
        row("g_mh"), full("w_dw"), row("b_dw"), row("g_cn"), row("b_cn"), full("w_out"),
    ]
    in_specs = [s for s, _ in specs_args]
    args = [a for _, a in specs_args]
    return pl.pallas_call(
        _mixer_kernel,
        out_shape=jax.ShapeDtypeStruct(x.shape, F32),
        grid=(batch, n_t),
        in_specs=in_specs,
        out_specs=pl.BlockSpec((1, TT, D_MODEL), lambda b, t: (b, t, 0)),
        scratch_shapes=[
            pltpu.VMEM((N_LANE_TILES, HIST_M + TT, LANES), F32),
            pltpu.VMEM((N_LANE_TILES, HIST_C + TT, LANES), F32),
            pltpu.VMEM((N_LANE_TILES, TT, LANES), F32),
            pltpu.VMEM((TT, D_MODEL), F32),
            pltpu.VMEM((TT, D_MODEL), F32),
            pltpu.VMEM((TT, D_MODEL), F32),
            pltpu.VMEM((TT, D_MODEL), F32),
            pltpu.VMEM((TT, D_MODEL), F32),
            pltpu.VMEM((N_LANE_TILES, TT, LANES), F32),
            pltpu.VMEM((N_HEADS, HEAD_DIM, HEAD_DIM), F32),
            pltpu.VMEM((N_HEADS, 1, HEAD_DIM), F32),
            pltpu.VMEM((N_HEADS, SUBLANES, LANES), F32),
        ],
        compiler_params=pltpu.CompilerParams(
            dimension_semantics=("arbitrary", "arbitrary"),
            vmem_limit_bytes=VMEM_LIMIT_BYTES),
        name="mixer",
    )(*args)


def _mlp_kernel(x_ref, mod_ref, gpre_ref, gpost_ref, w1_ref, w2_ref, o_ref):
    x = x_ref[0]
    sh = mod_ref[0]
    sc = mod_ref[1]
    gt = mod_ref[2]
    h = x * _rms_scale(x) * (gpre_ref[...] * (1.0 + sc)) + sh
    hb = h.astype(BF16)
    f = jnp.zeros((TM, D_MODEL), F32)
    for c in range(D_FF // D_MODEL):
        cs = slice(c * D_MODEL, (c + 1) * D_MODEL)
        u = jnp.maximum(_dot(hb, w1_ref[:, cs]), 0.0)
        f = f + _dot((u * u).astype(BF16), w2_ref[cs, :])
    o_ref[0] = x + gt * (f * _rms_scale(f) * gpost_ref[...])


def _mlp_call(layer, x, mod, p):
    batch, seq, _ = x.shape
    n_t = seq // TM
    row = lambda b, t: (layer, 0, 0)
    return pl.pallas_call(
        _mlp_kernel,
        out_shape=jax.ShapeDtypeStruct(x.shape, F32),
        grid=(batch, n_t),
        in_specs=[
            pl.BlockSpec((1, TM, D_MODEL), lambda b, t: (b, t, 0)),
            pl.BlockSpec((None, 3, None, 1, D_MODEL), lambda b, t: (layer, 0, b, 0, 0)),
            pl.BlockSpec((None, 1, D_MODEL), row),
            pl.BlockSpec((None, 1, D_MODEL), row),
            pl.BlockSpec((None, D_MODEL, D_FF), row),
            pl.BlockSpec((None, D_FF, D_MODEL), row),
        ],
        out_specs=pl.BlockSpec((1, TM, D_MODEL), lambda b, t: (b, t, 0)),
        compiler_params=pltpu.CompilerParams(
            dimension_semantics=("arbitrary", "arbitrary"),
            vmem_limit_bytes=VMEM_LIMIT_BYTES),
        name="mlp",
    )(x, mod, p["g_pre_mlp"], p["g_post_mlp"], p["w_ff1"], p["w_ff2"])


def kernel(x, c, w_ada, b_ada, g_pre_mix, g_post_mix, g_pre_mlp, g_post_mlp, w_in, w_conv_m, b_conv_m, w_q, w_k, w_v, w_gates, b_gates, g_mh, w_dw, b_dw, g_cn, b_cn, w_out, w_ff1, w_ff2):
    depth = w_in.shape[0]
    batch = x.shape[0]
    assert x.shape[1] % TM == 0 and x.shape[1] % TT == 0 and TT % CHUNK == 0

    def rows(a):
        return a.reshape(depth, 1, a.shape[-1]).astype(F32)

    wg = w_gates.reshape(depth, 3, HEAD_DIM, N_HEADS, 2 * N_HEADS)
    wg = wg.transpose(0, 1, 3, 2, 4).reshape(depth, 3, D_MODEL, 2 * N_HEADS)
    wg = jnp.pad(wg, ((0, 0), (0, 0), (0, 0), (0, LANES - 2 * N_HEADS))).astype(BF16)
    bg = jnp.pad(b_gates.astype(F32), ((0, 0), (0, LANES - 2 * N_HEADS))).reshape(depth, 1, LANES)

    p = {
        "g_pre_mix": rows(g_pre_mix), "g_post_mix": rows(g_post_mix),
        "g_pre_mlp": rows(g_pre_mlp), "g_post_mlp": rows(g_post_mlp),
        "w_in": w_in.astype(BF16), "w_conv_m": w_conv_m.astype(F32), "b_conv_m": rows(b_conv_m),
        "w_q": w_q.astype(BF16), "w_k": w_k.astype(BF16), "w_v": w_v.astype(BF16),
        "w_gates": wg, "b_gates": bg, "g_mh": rows(g_mh),
        "w_dw": w_dw.astype(F32), "b_dw": rows(b_dw), "g_cn": rows(g_cn), "b_cn": rows(b_cn),
        "w_out": w_out.astype(BF16), "w_ff1": w_ff1.astype(BF16), "w_ff2": w_ff2.astype(BF16),
    }

    mod = _ada_call(c.astype(F32), w_ada.astype(F32), b_ada.astype(F32))
    mod = mod.reshape(depth, 2, 3, batch, 1, D_MODEL)

    for layer in range(depth):
        x = _mixer_call(layer, x, mod[:, 0], p)
        x = _mlp_call(layer, x, mod[:, 1], p)
    return x
```

```python
import functools

import jax
import jax.numpy as jnp
from jax import lax
from jax.experimental import pallas as pl
from jax.experimental.pallas import tpu as pltpu

D_MODEL = 1024
N_HEADS = 4
HEAD_DIM = 256
N_CONV_GROUPS = 8
CONV_GROUP = D_MODEL // N_CONV_GROUPS
K_M = 4
K_C = 31
CHUNK = 256
D_FF = 4 * D_MODEL
N_ADA = 6
EPS = 1e-6

SUBLANES = 8
LANES = 128
VMEM_LIMIT_BYTES = 56 * 1024 * 1024

TT = 256
TM = 512
HIST_M = SUBLANES
HIST_C = 4 * SUBLANES
CONV_ROWS = 64
N_LANE_TILES = D_MODEL // LANES
assert CONV_GROUP == LANES

BF16 = jnp.bfloat16
F32 = jnp.float32


def _sigmoid(x):
    return jax.nn.sigmoid(x)


def _log_sigmoid(x):
    return jnp.minimum(x, 0.0) - jnp.log1p(jnp.exp(-jnp.abs(x)))


def _rms_scale(x):
    return lax.rsqrt(jnp.mean(x * x, axis=-1, keepdims=True) + EPS)


def _dot(a, b):
    return jnp.dot(a, b, preferred_element_type=F32)


def _ada_kernel(c_ref, w_ref, b_ref, o_ref):
    c = c_ref[...]
    c_act = c * _sigmoid(c)
    o_ref[0, 0] = _dot(c_act, w_ref[0]) + b_ref[0, 0]


def _ada_call(c, w_ada, b_ada):
    depth = w_ada.shape[0]
    batch = c.shape[0]
    return pl.pallas_call(
        _ada_kernel,
        out_shape=jax.ShapeDtypeStruct((depth, N_ADA, batch, D_MODEL), F32),
        grid=(depth, N_ADA),
        in_specs=[
            pl.BlockSpec((batch, D_MODEL), lambda l, j: (0, 0)),
            pl.BlockSpec((1, D_MODEL, D_MODEL), lambda l, j: (l, 0, j)),
            pl.BlockSpec((1, 1, 1, D_MODEL), lambda l, j: (l, j, 0, 0)),
        ],
        out_specs=pl.BlockSpec((1, 1, batch, D_MODEL), lambda l, j: (l, j, 0, 0)),
        compiler_params=pltpu.CompilerParams(
            dimension_semantics=("arbitrary", "arbitrary"),
            vmem_limit_bytes=VMEM_LIMIT_BYTES),
        name="ada_mod",
    )(c, w_ada, b_ada.reshape(depth, N_ADA, 1, D_MODEL))


def _causal_dwconv(src_ref, hist, w_ref, b_ref, k, lt, post, dst_ref):
    cols = slice(lt * LANES, (lt + 1) * LANES)
    starts = [span * 2 * CONV_ROWS + phase for span in range(TT // (2 * CONV_ROWS)) for phase in range(2)]
    bias = jnp.broadcast_to(b_ref[:, cols], (CONV_ROWS, LANES))
    accs = [bias for _ in starts]
    for j in range(k):
        w = w_ref[j:j + 1, cols]
        for n, r0 in enumerate(starts):
            off = r0 + hist - (k - 1) + j
            accs[n] = accs[n] + w * src_ref[lt, pl.ds(off, CONV_ROWS, stride=2), :]
    for n, r0 in enumerate(starts):
        dst_ref[lt, pl.ds(r0, CONV_ROWS, stride=2), :] = post(accs[n], cols)


def _mixer_kernel(x_ref, mod_ref, gpre_ref, gpost_ref, win_ref, wcm_ref, bcm_ref,
                  wq_ref, wk_ref, wv_ref, wg_ref, bg_ref, gmh_ref, wdw_ref, bdw_ref,
                  gcn_ref, bcn_ref, wout_ref, o_ref,
                  xm_ref, a_ref, xc_ref, q_ref, k_ref, v_ref, z_ref, hm_ref, ac_ref,
                  c_state, n_state, m_state):
    t_idx = pl.program_id(1)

    @pl.when(t_idx == 0)
    def _():
        xm_ref[:, 0:HIST_M, :] = jnp.zeros((N_LANE_TILES, HIST_M, LANES), F32)
        a_ref[:, 0:HIST_C, :] = jnp.zeros((N_LANE_TILES, HIST_C, LANES), F32)
        c_state[...] = jnp.zeros(c_state.shape, F32)
        n_state[...] = jnp.zeros(n_state.shape, F32)
        m_state[...] = jnp.zeros(m_state.shape, F32)

    x = x_ref[0]
    sh = mod_ref[0]
    sc = mod_ref[1]
    gt = mod_ref[2]

    h = x * _rms_scale(x) * (gpre_ref[...] * (1.0 + sc)) + sh
    hb = h.astype(BF16)

    x_m = _dot(hb, win_ref[:, 0:D_MODEL])
    z_ref[...] = _dot(hb, win_ref[:, D_MODEL:2 * D_MODEL])
    glu_a = _dot(hb, win_ref[:, 2 * D_MODEL:3 * D_MODEL])
    glu_b = _dot(hb, win_ref[:, 3 * D_MODEL:4 * D_MODEL])
    a_val = glu_a * _sigmoid(glu_b)
    for lt in range(N_LANE_TILES):
        cols = slice(lt * LANES, (lt + 1) * LANES)
        xm_ref[lt, HIST_M:HIST_M + TT, :] = x_m[:, cols]
        a_ref[lt, HIST_C:HIST_C + TT, :] = a_val[:, cols]

    def swish(acc, cols):
        return acc * _sigmoid(acc)

    for lt in range(N_LANE_TILES):
        _causal_dwconv(xm_ref, HIST_M, wcm_ref, bcm_ref, K_M, lt, swish, xc_ref)
    xmb = x_m.astype(BF16)
    for hd in range(N_HEADS):
        sl = slice(hd * HEAD_DIM, (hd + 1) * HEAD_DIM)
        xcb = jnp.concatenate([xc_ref[2 * hd], xc_ref[2 * hd + 1]], axis=1).astype(BF16)
        q_ref[:, sl] = _dot(xcb, wq_ref[hd])
        k_ref[:, sl] = _dot(xcb, wk_ref[hd])
        v_ref[:, sl] = _dot(xmb[:, sl], wv_ref[hd])

    gates = (_dot(q_ref[...].astype(BF16), wg_ref[0])
             + _dot(k_ref[...].astype(BF16), wg_ref[1])
             + _dot(v_ref[...].astype(BF16), wg_ref[2]) + bg_ref[...])
    gates_t = gates.T
    g8 = gates_t[0:SUBLANES, :]
    lf8 = _log_sigmoid(g8)

    lane = lax.broadcasted_iota(jnp.int32, (SUBLANES, TT), 1)
    lane_in_chunk = lane % CHUNK
    cs8 = lf8
    shift = 1
    while shift < CHUNK:
        rolled = pltpu.roll(cs8, shift, axis=1)
        cs8 = cs8 + jnp.where(lane_in_chunk >= shift, rolled, 0.0)
        shift *= 2

    row_i = lax.broadcasted_iota(jnp.int32, (CHUNK, CHUNK), 0)
    col_i = lax.broadcasted_iota(jnp.int32, (CHUNK, CHUNK), 1)
    causal = col_i <= row_i
    diag = col_i == row_i

    for ck in range(TT // CHUNK):
        rows = slice(ck * CHUNK, (ck + 1) * CHUNK)
        for hd in range(N_HEADS):
            sl = slice(hd * HEAD_DIM, (hd + 1) * HEAD_DIM)
            ig_row = g8[hd:hd + 1, rows]
            lf_row = lf8[N_HEADS + hd:N_HEADS + hd + 1, rows]
            b_row = cs8[N_HEADS + hd:N_HEADS + hd + 1, rows]
            b_col = jnp.sum(jnp.where(causal, lf_row, 0.0), axis=1, keepdims=True)
            ig_col = jnp.sum(jnp.where(diag, ig_row, 0.0), axis=1, keepdims=True)
            b_last = jnp.sum(lf_row, axis=1, keepdims=True)
            m_prev = m_state[hd, 0:1, 0:1]

            logw = jnp.where(causal, b_col + (ig_row - b_row), -jnp.inf)
            g_col = b_col + m_prev
            m_t = jnp.maximum(g_col, jnp.max(logw, axis=1, keepdims=True))
            w = jnp.exp(logw - m_t)
            inter = jnp.exp(g_col - m_t)

            qf = q_ref[rows, sl]
            kf = k_ref[rows, sl] * (HEAD_DIM ** -0.5)
            vb = v_ref[rows, sl].astype(BF16)
            qb = qf.astype(BF16)
            s = lax.dot_general(qb, kf.astype(BF16), (((1,), (1,)), ((), ())),
                                preferred_element_type=F32) * w
            c_prev = c_state[hd]
            n_prev = n_state[hd]
            num = _dot(s.astype(BF16), vb) + inter * _dot(qb, c_prev.astype(BF16))
            den = (jnp.sum(s, axis=1, keepdims=True)
                   + inter * jnp.sum(qf * n_prev, axis=1, keepdims=True))
            hm_ref[rows, sl] = num / jnp.maximum(jnp.abs(den), jnp.exp(-m_t))

            a_col = b_last - b_col + ig_col
            m_new = jnp.maximum(b_last + m_prev, jnp.max(a_col, axis=0, keepdims=True))
            wk = jnp.exp(a_col - m_new)
            decay = jnp.exp(b_last + m_prev - m_new)
            kw = kf * wk
            c_state[hd] = decay * c_prev + lax.dot_general(
                kw.astype(BF16), vb, (((0,), (0,)), ((), ())), preferred_element_type=F32)
            n_state[hd] = decay * n_prev + jnp.sum(kw, axis=0, keepdims=True)
            m_state[hd] = jnp.broadcast_to(m_new, (SUBLANES, LANES))

    for hd in range(N_HEADS):
        sl = slice(hd * HEAD_DIM, (hd + 1) * HEAD_DIM)
        hh = hm_ref[:, sl]
        mu = jnp.mean(hh, axis=-1, keepdims=True)
        hc = hh - mu
        var = jnp.mean(hc * hc, axis=-1, keepdims=True)
        hn = hc * lax.rsqrt(var + EPS) * gmh_ref[:, sl]
        hm_ref[:, sl] = _sigmoid(z_ref[:, sl]) * hn

    def ln_swish(acc, cols):
        mu = jnp.mean(acc, axis=-1, keepdims=True)
        ctr = acc - mu
        var = jnp.mean(ctr * ctr, axis=-1, keepdims=True)
        y = ctr * lax.rsqrt(var + EPS) * gcn_ref[:, cols] + bcn_ref[:, cols]
        return y * _sigmoid(y)

    for lt in range(N_LANE_TILES):
        _causal_dwconv(a_ref, HIST_C, wdw_ref, bdw_ref, K_C, lt, ln_swish, ac_ref)

    ac = jnp.concatenate([ac_ref[lt] for lt in range(N_LANE_TILES)], axis=1)
    y = (_dot(hm_ref[...].astype(BF16), wout_ref[0:D_MODEL, :])
         + _dot(ac.astype(BF16), wout_ref[D_MODEL:2 * D_MODEL, :]))
    o_ref[0] = x + gt * (y * _rms_scale(y) * gpost_ref[...])

    xm_ref[:, 0:HIST_M, :] = xm_ref[:, TT:HIST_M + TT, :]
    a_ref[:, 0:HIST_C, :] = a_ref[:, TT:HIST_C + TT, :]


def _mixer_call(layer, x, mod, p):
    batch, seq, _ = x.shape
    n_t = seq // TT

    def row(name):
        return pl.BlockSpec((None, 1, D_MODEL), lambda b, t: (layer, 0, 0)), p[name]

    def full(name):
        arr = p[name]
        blk = (None,) + arr.shape[1:]
        nd = arr.ndim - 1
        return pl.BlockSpec(blk, lambda b, t: (layer,) + (0,) * nd), arr

    specs_args = [
        (pl.BlockSpec((1, TT, D_MODEL), lambda b, t: (b, t, 0)), x),
        (pl.BlockSpec((None, 3, None, 1, D_MODEL), lambda b, t: (layer, 0, b, 0, 0)), mod),
        row("g_pre_mix"), row("g_post_mix"), full("w_in"), full("w_conv_m"), row("b_conv_m"),
        full("w_q"), full("w_k"), full("w_v"), full("w_gates"),
        (pl.BlockSpec((None, 1, LANES), lambda b, t: (layer, 0, 0)), p["b_gates"]),
        row("g_mh"), full("w_dw"), row("b_dw"), row("g_cn"), row("b_cn"), full("w_out"),
    ]
    in_specs = [s for s, _ in specs_args]
    args = [a for _, a in specs_args]
    return pl.pallas_call(
        _mixer_kernel,
        out_shape=jax.ShapeDtypeStruct(x.shape, F32),
        grid=(batch, n_t),
        in_specs=in_specs,
        out_specs=pl.BlockSpec((1, TT, D_MODEL), lambda b, t: (b, t, 0)),
        scratch_shapes=[
            pltpu.VMEM((N_LANE_TILES, HIST_M + TT, LANES), F32),
            pltpu.VMEM((N_LANE_TILES, HIST_C + TT, LANES), F32),
            pltpu.VMEM((N_LANE_TILES, TT, LANES), F32),
            pltpu.VMEM((TT, D_MODEL), F32),
            pltpu.VMEM((TT, D_MODEL), F32),
            pltpu.VMEM((TT, D_MODEL), F32),
            pltpu.VMEM((TT, D_MODEL), F32),
            pltpu.VMEM((TT, D_MODEL), F32),
            pltpu.VMEM((N_LANE_TILES, TT, LANES), F32),
            pltpu.VMEM((N_HEADS, HEAD_DIM, HEAD_DIM), F32),
            pltpu.VMEM((N_HEADS, 1, HEAD_DIM), F32),
            pltpu.VMEM((N_HEADS, SUBLANES, LANES), F32),
        ],
        compiler_params=pltpu.CompilerParams(
            dimension_semantics=("arbitrary", "arbitrary"),
            vmem_limit_bytes=VMEM_LIMIT_BYTES),
        name="mixer",
    )(*args)


def _mlp_kernel(x_ref, mod_ref, gpre_ref, gpost_ref, w1_ref, w2_ref, o_ref):
    x = x_ref[0]
    sh = mod_ref[0]
    sc = mod_ref[1]
    gt = mod_ref[2]
    h = x * _rms_scale(x) * (gpre_ref[...] * (1.0 + sc)) + sh
    hb = h.astype(BF16)
    f = jnp.zeros((TM, D_MODEL), F32)
    for c in range(D_FF // D_MODEL):
        cs = slice(c * D_MODEL, (c + 1) * D_MODEL)
        u = jnp.maximum(_dot(hb, w1_ref[:, cs]), 0.0)
        f = f + _dot((u * u).astype(BF16), w2_ref[cs, :])
    o_ref[0] = x + gt * (f * _rms_scale(f) * gpost_ref[...])


def _mlp_call(layer, x, mod, p):
    batch, seq, _ = x.shape
    n_t = seq // TM
    row = lambda b, t: (layer, 0, 0)
    return pl.pallas_call(
        _mlp_kernel,
        out_shape=jax.ShapeDtypeStruct(x.shape, F32),
        grid=(batch, n_t),
        in_specs=[
            pl.BlockSpec((1, TM, D_MODEL), lambda b, t: (b, t, 0)),
            pl.BlockSpec((None, 3, None, 1, D_MODEL), lambda b, t: (layer, 0, b, 0, 0)),
            pl.BlockSpec((None, 1, D_MODEL), row),
            pl.BlockSpec((None, 1, D_MODEL), row),
            pl.BlockSpec((None, D_MODEL, D_FF), row),
            pl.BlockSpec((None, D_FF, D_MODEL), row),
        ],
        out_specs=pl.BlockSpec((1, TM, D_MODEL), lambda b, t: (b, t, 0)),
        compiler_params=pltpu.CompilerParams(
            dimension_semantics=("arbitrary", "arbitrary"),
            vmem_limit_bytes=VMEM_LIMIT_BYTES),
        name="mlp",
    )(x, mod, p["g_pre_mlp"], p["g_post_mlp"], p["w_ff1"], p["w_ff2"])


def kernel(x, c, w_ada, b_ada, g_pre_mix, g_post_mix, g_pre_mlp, g_post_mlp, w_in, w_conv_m, b_conv_m, w_q, w_k, w_v, w_gates, b_gates, g_mh, w_dw, b_dw, g_cn, b_cn, w_out, w_ff1, w_ff2):
    depth = w_in.shape[0]
    batch = x.shape[0]
    assert x.shape[1] % TM == 0 and x.shape[1] % TT == 0 and TT % CHUNK == 0

    def rows(a):
        return a.reshape(depth, 1, a.shape[-1]).astype(F32)

    wg = w_gates.reshape(depth, 3, HEAD_DIM, N_HEADS, 2 * N_HEADS)
    wg = wg.transpose(0, 1, 3, 2, 4).reshape(depth, 3, D_MODEL, 2 * N_HEADS)
    wg = jnp.pad(wg, ((0, 0), (0, 0), (0, 0), (0, LANES - 2 * N_HEADS))).astype(BF16)
    bg = jnp.pad(b_gates.astype(F32), ((0, 0), (0, LANES - 2 * N_HEADS))).reshape(depth, 1, LANES)

    p = {
        "g_pre_mix": rows(g_pre_mix), "g_post_mix": rows(g_post_mix),
        "g_pre_mlp": rows(g_pre_mlp), "g_post_mlp": rows(g_post_mlp),
        "w_in": w_in.astype(BF16), "w_conv_m": w_conv_m.astype(F32), "b_conv_m": rows(b_conv_m),
        "w_q": w_q.astype(BF16), "w_k": w_k.astype(BF16), "w_v": w_v.astype(BF16),
        "w_gates": wg, "b_gates": bg, "g_mh": rows(g_mh),
        "w_dw": w_dw.astype(F32), "b_dw": rows(b_dw), "g_cn": rows(g_cn), "b_cn": rows(b_cn),
        "w_out": w_out.astype(BF16), "w_ff1": w_ff1.astype(BF16), "w_ff2": w_ff2.astype(BF16),
    }

    mod = _ada_call(c.astype(F32), w_ada.astype(F32), b_ada.astype(F32))
    mod = mod.reshape(depth, 2, 3, batch, 1, D_MODEL)

    for layer in range(depth):
        x = _mixer_call(layer, x, mod[:, 0], p)
        x = _mlp_call(layer, x, mod[:, 1], p)
    return x
```

```python
import functools

import jax
import jax.numpy as jnp
from jax import lax
from jax.experimental import pallas as pl
from jax.experimental.pallas import tpu as pltpu

D_MODEL = 1024
N_HEADS = 4
HEAD_DIM = 256
N_CONV_GROUPS = 8
CONV_GROUP = D_MODEL // N_CONV_GROUPS
K_M = 4
K_C = 31
CHUNK = 256
D_FF = 4 * D_MODEL
N_ADA = 6
EPS = 1e-6

SUBLANES = 8
LANES = 128
VMEM_LIMIT_BYTES = 56 * 1024 * 1024

TT = 512
TM = 512
HIST_M = SUBLANES
HIST_C = 4 * SUBLANES
CONV_ROWS = 64
N_LANE_TILES = D_MODEL // LANES
assert CONV_GROUP == LANES

BF16 = jnp.bfloat16
F32 = jnp.float32


def _sigmoid(x):
    return jax.nn.sigmoid(x)


def _log_sigmoid(x):
    return jnp.minimum(x, 0.0) - jnp.log1p(jnp.exp(-jnp.abs(x)))


def _rms_scale(x):
    return lax.rsqrt(jnp.mean(x * x, axis=-1, keepdims=True) + EPS)


def _dot(a, b):
    return jnp.dot(a, b, preferred_element_type=F32)


def _ada_kernel(c_ref, w_ref, b_ref, o_ref):
    c = c_ref[...]
    c_act = c * _sigmoid(c)
    o_ref[0, 0] = _dot(c_act, w_ref[0]) + b_ref[0, 0]


def _ada_call(c, w_ada, b_ada):
    depth = w_ada.shape[0]
    batch = c.shape[0]
    return pl.pallas_call(
        _ada_kernel,
        out_shape=jax.ShapeDtypeStruct((depth, N_ADA, batch, D_MODEL), F32),
        grid=(depth, N_ADA),
        in_specs=[
            pl.BlockSpec((batch, D_MODEL), lambda l, j: (0, 0)),
            pl.BlockSpec((1, D_MODEL, D_MODEL), lambda l, j: (l, 0, j)),
            pl.BlockSpec((1, 1, 1, D_MODEL), lambda l, j: (l, j, 0, 0)),
        ],
        out_specs=pl.BlockSpec((1, 1, batch, D_MODEL), lambda l, j: (l, j, 0, 0)),
        compiler_params=pltpu.CompilerParams(
            dimension_semantics=("arbitrary", "arbitrary"),
            vmem_limit_bytes=VMEM_LIMIT_BYTES),
        name="ada_mod",
    )(c, w_ada, b_ada.reshape(depth, N_ADA, 1, D_MODEL))


def _causal_dwconv(src_ref, hist, w_ref, b_ref, k, lt, post, dst_ref):
    cols = slice(lt * LANES, (lt + 1) * LANES)
    starts = [span * 2 * CONV_ROWS + phase for span in range(TT // (2 * CONV_ROWS)) for phase in range(2)]
    bias = jnp.broadcast_to(b_ref[:, cols], (CONV_ROWS, LANES))
    accs = [bias for _ in starts]
    for j in range(k):
        w = w_ref[j:j + 1, cols]
        for n, r0 in enumerate(starts):
            off = r0 + hist - (k - 1) + j
            accs[n] = accs[n] + w * src_ref[lt, pl.ds(off, CONV_ROWS, stride=2), :]
    for n, r0 in enumerate(starts):
        dst_ref[lt, pl.ds(r0, CONV_ROWS, stride=2), :] = post(accs[n], cols)


def _mixer_kernel(x_ref, mod_ref, gpre_ref, gpost_ref, win_ref, wcm_ref, bcm_ref,
                  wq_ref, wk_ref, wv_ref, wg_ref, bg_ref, gmh_ref, wdw_ref, bdw_ref,
                  gcn_ref, bcn_ref, wout_ref, o_ref,
                  xm_ref, a_ref, xc_ref, q_ref, k_ref, v_ref, z_ref, hm_ref, ac_ref,
                  c_state, n_state, m_state):
    t_idx = pl.program_id(1)

    @pl.when(t_idx == 0)
    def _():
        xm_ref[:, 0:HIST_M, :] = jnp.zeros((N_LANE_TILES, HIST_M, LANES), F32)
        a_ref[:, 0:HIST_C, :] = jnp.zeros((N_LANE_TILES, HIST_C, LANES), F32)
        c_state[...] = jnp.zeros(c_state.shape, F32)
        n_state[...] = jnp.zeros(n_state.shape, F32)
        m_state[...] = jnp.zeros(m_state.shape, F32)

    x = x_ref[0]
    sh = mod_ref[0]
    sc = mod_ref[1]
    gt = mod_ref[2]

    h = x * _rms_scale(x) * (gpre_ref[...] * (1.0 + sc)) + sh
    hb = h.astype(BF16)

    for hd in range(N_HEADS):
        ca = slice(2 * D_MODEL + hd * HEAD_DIM, 2 * D_MODEL + (hd + 1) * HEAD_DIM)
        cb = slice(3 * D_MODEL + hd * HEAD_DIM, 3 * D_MODEL + (hd + 1) * HEAD_DIM)
        a_val = _dot(hb, win_ref[:, ca]) * _sigmoid(_dot(hb, win_ref[:, cb]))
        for half in range(HEAD_DIM // LANES):
            a_ref[2 * hd + half, HIST_C:HIST_C + TT, :] = a_val[:, half * LANES:(half + 1) * LANES]
    x_m = _dot(hb, win_ref[:, 0:D_MODEL])
    for lt in range(N_LANE_TILES):
        xm_ref[lt, HIST_M:HIST_M + TT, :] = x_m[:, lt * LANES:(lt + 1) * LANES]
    z_ref[...] = _dot(hb, win_ref[:, D_MODEL:2 * D_MODEL])

    def swish(acc, cols):
        return acc * _sigmoid(acc)

    for lt in range(N_LANE_TILES):
        _causal_dwconv(xm_ref, HIST_M, wcm_ref, bcm_ref, K_M, lt, swish, xc_ref)
    xmb = x_m.astype(BF16)
    for hd in range(N_HEADS):
        sl = slice(hd * HEAD_DIM, (hd + 1) * HEAD_DIM)
        xcb = jnp.concatenate([xc_ref[2 * hd], xc_ref[2 * hd + 1]], axis=1).astype(BF16)
        q_ref[:, sl] = _dot(xcb, wq_ref[hd])
        k_ref[:, sl] = _dot(xcb, wk_ref[hd])
        v_ref[:, sl] = _dot(xmb[:, sl], wv_ref[hd])

    gates = (_dot(q_ref[...].astype(BF16), wg_ref[0])
             + _dot(k_ref[...].astype(BF16), wg_ref[1])
             + _dot(v_ref[...].astype(BF16), wg_ref[2]) + bg_ref[...])
    gates_t = gates.T
    g8 = gates_t[0:SUBLANES, :]
    lf8 = _log_sigmoid(g8)

    lane = lax.broadcasted_iota(jnp.int32, (SUBLANES, TT), 1)
    lane_in_chunk = lane % CHUNK
    cs8 = lf8
    shift = 1
    while shift < CHUNK:
        rolled = pltpu.roll(cs8, shift, axis=1)
        cs8 = cs8 + jnp.where(lane_in_chunk >= shift, rolled, 0.0)
        shift *= 2

    row_i = lax.broadcasted_iota(jnp.int32, (CHUNK, CHUNK), 0)
    col_i = lax.broadcasted_iota(jnp.int32, (CHUNK, CHUNK), 1)
    causal = col_i <= row_i
    diag = col_i == row_i

    for ck in range(TT // CHUNK):
        rows = slice(ck * CHUNK, (ck + 1) * CHUNK)
        for hd in range(N_HEADS):
            sl = slice(hd * HEAD_DIM, (hd + 1) * HEAD_DIM)
            ig_row = g8[hd:hd + 1, rows]
            lf_row = lf8[N_HEADS + hd:N_HEADS + hd + 1, rows]
            b_row = cs8[N_HEADS + hd:N_HEADS + hd + 1, rows]
            b_col = jnp.sum(jnp.where(causal, lf_row, 0.0), axis=1, keepdims=True)
            ig_col = jnp.sum(jnp.where(diag, ig_row, 0.0), axis=1, keepdims=True)
            b_last = jnp.sum(lf_row, axis=1, keepdims=True)
            m_prev = m_state[hd, 0:1, 0:1]

            logw = jnp.where(causal, b_col + (ig_row - b_row), -jnp.inf)
            g_col = b_col + m_prev
            m_t = jnp.maximum(g_col, jnp.max(logw, axis=1, keepdims=True))
            w = jnp.exp(logw - m_t)
            inter = jnp.exp(g_col - m_t)

            qf = q_ref[rows, sl]
            kf = k_ref[rows, sl] * (HEAD_DIM ** -0.5)
            vb = v_ref[rows, sl].astype(BF16)
            qb = qf.astype(BF16)
            s = lax.dot_general(qb, kf.astype(BF16), (((1,), (1,)), ((), ())),
                                preferred_element_type=F32) * w
            c_prev = c_state[hd]
            n_prev = n_state[hd]
            num = _dot(s.astype(BF16), vb) + inter * _dot(qb, c_prev.astype(BF16))
            den = (jnp.sum(s, axis=1, keepdims=True)
                   + inter * jnp.sum(qf * n_prev, axis=1, keepdims=True))
            hm_ref[rows, sl] = num / jnp.maximum(jnp.abs(den), jnp.exp(-m_t))

            a_col = b_last - b_col + ig_col
            m_new = jnp.maximum(b_last + m_prev, jnp.max(a_col, axis=0, keepdims=True))
            wk = jnp.exp(a_col - m_new)
            decay = jnp.exp(b_last + m_prev - m_new)
            kw = kf * wk
            c_state[hd] = decay * c_prev + lax.dot_general(
                kw.astype(BF16), vb, (((0,), (0,)), ((), ())), preferred_element_type=F32)
            n_state[hd] = decay * n_prev + jnp.sum(kw, axis=0, keepdims=True)
            m_state[hd] = jnp.broadcast_to(m_new, (SUBLANES, LANES))

    for hd in range(N_HEADS):
        sl = slice(hd * HEAD_DIM, (hd + 1) * HEAD_DIM)
        hh = hm_ref[:, sl]
        mu = jnp.mean(hh, axis=-1, keepdims=True)
        hc = hh - mu
        var = jnp.mean(hc * hc, axis=-1, keepdims=True)
        hn = hc * lax.rsqrt(var + EPS) * gmh_ref[:, sl]
        hm_ref[:, sl] = _sigmoid(z_ref[:, sl]) * hn

    def ln_swish(acc, cols):
        mu = jnp.mean(acc, axis=-1, keepdims=True)
        ctr = acc - mu
        var = jnp.mean(ctr * ctr, axis=-1, keepdims=True)
        y = ctr * lax.rsqrt(var + EPS) * gcn_ref[:, cols] + bcn_ref[:, cols]
        return y * _sigmoid(y)

    for lt in range(N_LANE_TILES):
        _causal_dwconv(a_ref, HIST_C, wdw_ref, bdw_ref, K_C, lt, ln_swish, ac_ref)

    ac = jnp.concatenate([ac_ref[lt] for lt in range(N_LANE_TILES)], axis=1)
    y = (_dot(hm_ref[...].astype(BF16), wout_ref[0:D_MODEL, :])
         + _dot(ac.astype(BF16), wout_ref[D_MODEL:2 * D_MODEL, :]))
    o_ref[0] = x + gt * (y * _rms_scale(y) * gpost_ref[...])

    xm_ref[:, 0:HIST_M, :] = xm_ref[:, TT:HIST_M + TT, :]
    a_ref[:, 0:HIST_C, :] = a_ref[:, TT:HIST_C + TT, :]


def _mixer_call(layer, x, mod, p):
    batch, seq, _ = x.shape
    n_t = seq // TT

    single = pl.Buffered(1)

    def row(name):
        return pl.BlockSpec((None, 1, D_MODEL), lambda b, t: (layer, 0, 0), pipeline_mode=single), p[name]

    def full(name):
        arr = p[name]
        blk = (None,) + arr.shape[1:]
        nd = arr.ndim - 1
        return pl.BlockSpec(blk, lambda b, t: (layer,) + (0,) * nd, pipeline_mode=single), arr

    specs_args = [
        (pl.BlockSpec((1, TT, D_MODEL), lambda b, t: (b, t, 0)), x),
        (pl.BlockSpec((None, 3, None, 1, D_MODEL), lambda b, t: (layer, 0, b, 0, 0)), mod),
        row("g_pre_mix"), row("g_post_mix"), full("w_in"), full("w_conv_m"), row("b_conv_m"),
        full("w_q"), full("w_k"), full("w_v"), full("w_gates"),
        (pl.BlockSpec((None, 1, LANES), lambda b, t: (layer, 0, 0)), p["b_gates"]),
        row("g_mh"), full("w_dw"), row("b_dw"), row("g_cn"), row("b_cn"), full("w_out"),
    ]
    in_specs = [s for s, _ in specs_args]
    args = [a for _, a in specs_args]
    return pl.pallas_call(
        _mixer_kernel,
        out_shape=jax.ShapeDtypeStruct(x.shape, F32),
        grid=(batch, n_t),
        in_specs=in_specs,
        out_specs=pl.BlockSpec((1, TT, D_MODEL), lambda b, t: (b, t, 0)),
        scratch_shapes=[
            pltpu.VMEM((N_LANE_TILES, HIST_M + TT, LANES), F32),
            pltpu.VMEM((N_LANE_TILES, HIST_C + TT, LANES), F32),
            pltpu.VMEM((N_LANE_TILES, TT, LANES), F32),
            pltpu.VMEM((TT, D_MODEL), F32),
            pltpu.VMEM((TT, D_MODEL), F32),
            pltpu.VMEM((TT, D_MODEL), F32),
            pltpu.VMEM((TT, D_MODEL), F32),
            pltpu.VMEM((TT, D_MODEL), F32),
            pltpu.VMEM((N_LANE_TILES, TT, LANES), F32),
            pltpu.VMEM((N_HEADS, HEAD_DIM, HEAD_DIM), F32),
            pltpu.VMEM((N_HEADS, 1, HEAD_DIM), F32),
            pltpu.VMEM((N_HEADS, SUBLANES, LANES), F32),
        ],
        compiler_params=pltpu.CompilerParams(
            dimension_semantics=("arbitrary", "arbitrary"),
            vmem_limit_bytes=VMEM_LIMIT_BYTES),
        name="mixer",
    )(*args)


def _mlp_kernel(x_ref, mod_ref, gpre_ref, gpost_ref, w1_ref, w2_ref, o_ref):
    x = x_ref[0]
    sh = mod_ref[0]
    sc = mod_ref[1]
    gt = mod_ref[2]
    h = x * _rms_scale(x) * (gpre_ref[...] * (1.0 + sc)) + sh
    hb = h.astype(BF16)
    f = jnp.zeros((TM, D_MODEL), F32)
    for c in range(D_FF // D_MODEL):
        cs = slice(c * D_MODEL, (c + 1) * D_MODEL)
        u = jnp.maximum(_dot(hb, w1_ref[:, cs]), 0.0)
        f = f + _dot((u * u).astype(BF16), w2_ref[cs, :])
    o_ref[0] = x + gt * (f * _rms_scale(f) * gpost_ref[...])


def _mlp_call(layer, x, mod, p):
    batch, seq, _ = x.shape
    n_t = seq // TM
    row = lambda b, t: (layer, 0, 0)
    return pl.pallas_call(
        _mlp_kernel,
        out_shape=jax.ShapeDtypeStruct(x.shape, F32),
        grid=(batch, n_t),
        in_specs=[
            pl.BlockSpec((1, TM, D_MODEL), lambda b, t: (b, t, 0)),
            pl.BlockSpec((None, 3, None, 1, D_MODEL), lambda b, t: (layer, 0, b, 0, 0)),
            pl.BlockSpec((None, 1, D_MODEL), row),
            pl.BlockSpec((None, 1, D_MODEL), row),
            pl.BlockSpec((None, D_MODEL, D_FF), row),
            pl.BlockSpec((None, D_FF, D_MODEL), row),
        ],
        out_specs=pl.BlockSpec((1, TM, D_MODEL), lambda b, t: (b, t, 0)),
        compiler_params=pltpu.CompilerParams(
            dimension_semantics=("arbitrary", "arbitrary"),
            vmem_limit_bytes=VMEM_LIMIT_BYTES),
        name="mlp",
    )(x, mod, p["g_pre_mlp"], p["g_post_mlp"], p["w_ff1"], p["w_ff2"])


def kernel(x, c, w_ada, b_ada, g_pre_mix, g_post_mix, g_pre_mlp, g_post_mlp, w_in, w_conv_m, b_conv_m, w_q, w_k, w_v, w_gates, b_gates, g_mh, w_dw, b_dw, g_cn, b_cn, w_out, w_ff1, w_ff2):
    depth = w_in.shape[0]
    batch = x.shape[0]
    assert x.shape[1] % TM == 0 and x.shape[1] % TT == 0 and TT % CHUNK == 0

    def rows(a):
        return a.reshape(depth, 1, a.shape[-1]).astype(F32)

    wg = w_gates.reshape(depth, 3, HEAD_DIM, N_HEADS, 2 * N_HEADS)
    wg = wg.transpose(0, 1, 3, 2, 4).reshape(depth, 3, D_MODEL, 2 * N_HEADS)
    wg = jnp.pad(wg, ((0, 0), (0, 0), (0, 0), (0, LANES - 2 * N_HEADS))).astype(BF16)
    bg = jnp.pad(b_gates.astype(F32), ((0, 0), (0, LANES - 2 * N_HEADS))).reshape(depth, 1, LANES)

    p = {
        "g_pre_mix": rows(g_pre_mix), "g_post_mix": rows(g_post_mix),
        "g_pre_mlp": rows(g_pre_mlp), "g_post_mlp": rows(g_post_mlp),
        "w_in": w_in.astype(BF16), "w_conv_m": w_conv_m.astype(F32), "b_conv_m": rows(b_conv_m),
        "w_q": w_q.astype(BF16), "w_k": w_k.astype(BF16), "w_v": w_v.astype(BF16),
        "w_gates": wg, "b_gates": bg, "g_mh": rows(g_mh),
        "w_dw": w_dw.astype(F32), "b_dw": rows(b_dw), "g_cn": rows(g_cn), "b_cn": rows(b_cn),
        "w_out": w_out.astype(BF16), "w_ff1": w_ff1.astype(BF16), "w_ff2": w_ff2.astype(BF16),
    }

    mod = _ada_call(c.astype(F32), w_ada.astype(F32), b_ada.astype(F32))
    mod = mod.reshape(depth, 2, 3, batch, 1, D_MODEL)

    for layer in range(depth):
        x = _mixer_call(layer, x, mod[:, 0], p)
        x = _mlp_call(layer, x, mod[:, 1], p)
    return x
```

```python
import functools

import jax
import jax.numpy as jnp
from jax import lax
from jax.experimental import pallas as pl
from jax.experimental.pallas import tpu as pltpu

D_MODEL = 1024
N_HEADS = 4
HEAD_DIM = 256
N_CONV_GROUPS = 8
CONV_GROUP = D_MODEL // N_CONV_GROUPS
K_M = 4
K_C = 31
CHUNK = 512
D_FF = 4 * D_MODEL
N_ADA = 6
EPS = 1e-6

SUBLANES = 8
LANES = 128
VMEM_LIMIT_BYTES = 56 * 1024 * 1024

TT = 512
TM = 1024
TM_HALF = TM // 2
HIST_M = SUBLANES
HIST_C = 4 * SUBLANES
CONV_ROWS = 64
N_LANE_TILES = D_MODEL // LANES
assert CONV_GROUP == LANES

BF16 = jnp.bfloat16
F32 = jnp.float32


def _sigmoid(x):
    return jax.nn.sigmoid(x)


def _log_sigmoid(x):
    return jnp.minimum(x, 0.0) - jnp.log1p(jnp.exp(-jnp.abs(x)))


def _rms_scale(x):
    return lax.rsqrt(jnp.mean(x * x, axis=-1, keepdims=True) + EPS)


def _dot(a, b):
    return jnp.dot(a, b, preferred_element_type=F32)


def _ada_kernel(c_ref, w_ref, b_ref, o_ref):
    c = c_ref[...]
    c_act = c * _sigmoid(c)
    o_ref[0, 0] = _dot(c_act, w_ref[0]) + b_ref[0, 0]


def _ada_call(c, w_ada, b_ada):
    depth = w_ada.shape[0]
    batch = c.shape[0]
    return pl.pallas_call(
        _ada_kernel,
        out_shape=jax.ShapeDtypeStruct((depth, N_ADA, batch, D_MODEL), F32),
        grid=(depth, N_ADA),
        in_specs=[
            pl.BlockSpec((batch, D_MODEL), lambda l, j: (0, 0)),
            pl.BlockSpec((1, D_MODEL, D_MODEL), lambda l, j: (l, 0, j)),
            pl.BlockSpec((1, 1, 1, D_MODEL), lambda l, j: (l, j, 0, 0)),
        ],
        out_specs=pl.BlockSpec((1, 1, batch, D_MODEL), lambda l, j: (l, j, 0, 0)),
        compiler_params=pltpu.CompilerParams(
            dimension_semantics=("arbitrary", "arbitrary"),
            vmem_limit_bytes=VMEM_LIMIT_BYTES),
        name="ada_mod",
    )(c, w_ada, b_ada.reshape(depth, N_ADA, 1, D_MODEL))


def _causal_dwconv(src_ref, hist, w_ref, b_ref, k, lt, post, dst_ref):
    cols = slice(lt * LANES, (lt + 1) * LANES)
    starts = [span * 2 * CONV_ROWS + phase for span in range(TT // (2 * CONV_ROWS)) for phase in range(2)]
    bias = jnp.broadcast_to(b_ref[:, cols], (CONV_ROWS, LANES))
    accs = [bias for _ in starts]
    for j in range(k):
        w = w_ref[j:j + 1, cols]
        for n, r0 in enumerate(starts):
            off = r0 + hist - (k - 1) + j
            accs[n] = accs[n] + w * src_ref[lt, pl.ds(off, CONV_ROWS, stride=2), :]
    for n, r0 in enumerate(starts):
        dst_ref[lt, pl.ds(r0, CONV_ROWS, stride=2), :] = post(accs[n], cols)


def _mixer_kernel(x_ref, mod_ref, gpre_ref, gpost_ref, win_ref, wcm_ref, bcm_ref,
                  wq_ref, wk_ref, wv_ref, wg_ref, bg_ref, gmh_ref, wdw_ref, bdw_ref,
                  gcn_ref, bcn_ref, wout_ref, o_ref,
                  xm_ref, a_ref, xc_ref, q_ref, k_ref, v_ref, z_ref, hm_ref, ac_ref,
                  c_state, n_state, m_state):
    t_idx = pl.program_id(1)

    @pl.when(t_idx == 0)
    def _():
        xm_ref[:, 0:HIST_M, :] = jnp.zeros((N_LANE_TILES, HIST_M, LANES), F32)
        a_ref[:, 0:HIST_C, :] = jnp.zeros((N_LANE_TILES, HIST_C, LANES), F32)
        c_state[...] = jnp.zeros(c_state.shape, F32)
        n_state[...] = jnp.zeros(n_state.shape, F32)
        m_state[...] = jnp.zeros(m_state.shape, F32)

    x = x_ref[0]
    sh = mod_ref[0]
    sc = mod_ref[1]
    gt = mod_ref[2]

    h = x * _rms_scale(x) * (gpre_ref[...] * (1.0 + sc)) + sh
    hb = h.astype(BF16)

    for hd in range(N_HEADS):
        ca = slice(2 * D_MODEL + hd * HEAD_DIM, 2 * D_MODEL + (hd + 1) * HEAD_DIM)
        cb = slice(3 * D_MODEL + hd * HEAD_DIM, 3 * D_MODEL + (hd + 1) * HEAD_DIM)
        a_val = _dot(hb, win_ref[:, ca]) * _sigmoid(_dot(hb, win_ref[:, cb]))
        for half in range(HEAD_DIM // LANES):
            a_ref[2 * hd + half, HIST_C:HIST_C + TT, :] = a_val[:, half * LANES:(half + 1) * LANES]
    x_m = _dot(hb, win_ref[:, 0:D_MODEL])
    for lt in range(N_LANE_TILES):
        xm_ref[lt, HIST_M:HIST_M + TT, :] = x_m[:, lt * LANES:(lt + 1) * LANES]
    z_ref[...] = _dot(hb, win_ref[:, D_MODEL:2 * D_MODEL])

    def swish(acc, cols):
        return acc * _sigmoid(acc)

    for lt in range(N_LANE_TILES):
        _causal_dwconv(xm_ref, HIST_M, wcm_ref, bcm_ref, K_M, lt, swish, xc_ref)
    xmb = x_m.astype(BF16)
    for hd in range(N_HEADS):
        sl = slice(hd * HEAD_DIM, (hd + 1) * HEAD_DIM)
        xcb = jnp.concatenate([xc_ref[2 * hd], xc_ref[2 * hd + 1]], axis=1).astype(BF16)
        q_ref[:, sl] = _dot(xcb, wq_ref[hd])
        k_ref[:, sl] = _dot(xcb, wk_ref[hd])
        v_ref[:, sl] = _dot(xmb[:, sl], wv_ref[hd])

    gates = (_dot(q_ref[...].astype(BF16), wg_ref[0])
             + _dot(k_ref[...].astype(BF16), wg_ref[1])
             + _dot(v_ref[...].astype(BF16), wg_ref[2]) + bg_ref[...])
    gates_t = gates.T
    g8 = gates_t[0:SUBLANES, :]
    lf8 = _log_sigmoid(g8)

    lane = lax.broadcasted_iota(jnp.int32, (SUBLANES, TT), 1)
    lane_in_chunk = lane % CHUNK
    cs8 = lf8
    shift = 1
    while shift < CHUNK:
        rolled = pltpu.roll(cs8, shift, axis=1)
        cs8 = cs8 + jnp.where(lane_in_chunk >= shift, rolled, 0.0)
        shift *= 2

    row_i = lax.broadcasted_iota(jnp.int32, (CHUNK, CHUNK), 0)
    col_i = lax.broadcasted_iota(jnp.int32, (CHUNK, CHUNK), 1)
    causal = col_i <= row_i
    diag = col_i == row_i

    for ck in range(TT // CHUNK):
        rows = slice(ck * CHUNK, (ck + 1) * CHUNK)
        for hd in range(N_HEADS):
            sl = slice(hd * HEAD_DIM, (hd + 1) * HEAD_DIM)
            ig_row = g8[hd:hd + 1, rows]
            lf_row = lf8[N_HEADS + hd:N_HEADS + hd + 1, rows]
            b_row = cs8[N_HEADS + hd:N_HEADS + hd + 1, rows]
            b_col = jnp.sum(jnp.where(causal, lf_row, 0.0), axis=1, keepdims=True)
            ig_col = jnp.sum(jnp.where(diag, ig_row, 0.0), axis=1, keepdims=True)
            b_last = jnp.sum(lf_row, axis=1, keepdims=True)
            m_prev = m_state[hd, 0:1, 0:1]

            logw = jnp.where(causal, b_col + (ig_row - b_row), -jnp.inf)
            g_col = b_col + m_prev
            m_t = jnp.maximum(g_col, jnp.max(logw, axis=1, keepdims=True))
            w = jnp.exp(logw - m_t)
            inter = jnp.exp(g_col - m_t)

            qf = q_ref[rows, sl]
            kf = k_ref[rows, sl] * (HEAD_DIM ** -0.5)
            vb = v_ref[rows, sl].astype(BF16)
            qb = qf.astype(BF16)
            s = lax.dot_general(qb, kf.astype(BF16), (((1,), (1,)), ((), ())),
                                preferred_element_type=F32) * w
            c_prev = c_state[hd]
            n_prev = n_state[hd]
            num = _dot(s.astype(BF16), vb) + inter * _dot(qb, c_prev.astype(BF16))
            den = (jnp.sum(s, axis=1, keepdims=True)
                   + inter * jnp.sum(qf * n_prev, axis=1, keepdims=True))
            hm_ref[rows, sl] = num / jnp.maximum(jnp.abs(den), jnp.exp(-m_t))

            a_col = b_last - b_col + ig_col
            m_new = jnp.maximum(b_last + m_prev, jnp.max(a_col, axis=0, keepdims=True))
            wk = jnp.exp(a_col - m_new)
            decay = jnp.exp(b_last + m_prev - m_new)
            kw = kf * wk
            c_state[hd] = decay * c_prev + lax.dot_general(
                kw.astype(BF16), vb, (((0,), (0,)), ((), ())), preferred_element_type=F32)
            n_state[hd] = decay * n_prev + jnp.sum(kw, axis=0, keepdims=True)
            m_state[hd] = jnp.broadcast_to(m_new, (SUBLANES, LANES))

    for hd in range(N_HEADS):
        sl = slice(hd * HEAD_DIM, (hd + 1) * HEAD_DIM)
        hh = hm_ref[:, sl]
        mu = jnp.mean(hh, axis=-1, keepdims=True)
        hc = hh - mu
        var = jnp.mean(hc * hc, axis=-1, keepdims=True)
        hn = hc * lax.rsqrt(var + EPS) * gmh_ref[:, sl]
        hm_ref[:, sl] = _sigmoid(z_ref[:, sl]) * hn

    def ln_swish(acc, cols):
        mu = jnp.mean(acc, axis=-1, keepdims=True)
        ctr = acc - mu
        var = jnp.mean(ctr * ctr, axis=-1, keepdims=True)
        y = ctr * lax.rsqrt(var + EPS) * gcn_ref[:, cols] + bcn_ref[:, cols]
        return y * _sigmoid(y)

    for lt in range(N_LANE_TILES):
        _causal_dwconv(a_ref, HIST_C, wdw_ref, bdw_ref, K_C, lt, ln_swish, ac_ref)

    ac = jnp.concatenate([ac_ref[lt] for lt in range(N_LANE_TILES)], axis=1)
    y = (_dot(hm_ref[...].astype(BF16), wout_ref[0:D_MODEL, :])
         + _dot(ac.astype(BF16), wout_ref[D_MODEL:2 * D_MODEL, :]))
    o_ref[0] = x + gt * (y * _rms_scale(y) * gpost_ref[...])

    xm_ref[:, 0:HIST_M, :] = xm_ref[:, TT:HIST_M + TT, :]
    a_ref[:, 0:HIST_C, :] = a_ref[:, TT:HIST_C + TT, :]


def _mixer_call(layer, x, mod, p):
    batch, seq, _ = x.shape
    n_t = seq // TT

    single = pl.Buffered(1)

    def row(name):
        return pl.BlockSpec((None, 1, D_MODEL), lambda b, t: (layer, 0, 0), pipeline_mode=single), p[name]

    def full(name):
        arr = p[name]
        blk = (None,) + arr.shape[1:]
        nd = arr.ndim - 1
        return pl.BlockSpec(blk, lambda b, t: (layer,) + (0,) * nd, pipeline_mode=single), arr

    specs_args = [
        (pl.BlockSpec((1, TT, D_MODEL), lambda b, t: (b, t, 0)), x),
        (pl.BlockSpec((None, 3, None, 1, D_MODEL), lambda b, t: (layer, 0, b, 0, 0)), mod),
        row("g_pre_mix"), row("g_post_mix"), full("w_in"), full("w_conv_m"), row("b_conv_m"),
        full("w_q"), full("w_k"), full("w_v"), full("w_gates"),
        (pl.BlockSpec((None, 1, LANES), lambda b, t: (layer, 0, 0)), p["b_gates"]),
        row("g_mh"), full("w_dw"), row("b_dw"), row("g_cn"), row("b_cn"), full("w_out"),
    ]
    in_specs = [s for s, _ in specs_args]
    args = [a for _, a in specs_args]
    return pl.pallas_call(
        _mixer_kernel,
        out_shape=jax.ShapeDtypeStruct(x.shape, F32),
        grid=(batch, n_t),
        in_specs=in_specs,
        out_specs=pl.BlockSpec((1, TT, D_MODEL), lambda b, t: (b, t, 0)),
        scratch_shapes=[
            pltpu.VMEM((N_LANE_TILES, HIST_M + TT, LANES), F32),
            pltpu.VMEM((N_LANE_TILES, HIST_C + TT, LANES), F32),
            pltpu.VMEM((N_LANE_TILES, TT, LANES), F32),
            pltpu.VMEM((TT, D_MODEL), F32),
            pltpu.VMEM((TT, D_MODEL), F32),
            pltpu.VMEM((TT, D_MODEL), F32),
            pltpu.VMEM((TT, D_MODEL), F32),
            pltpu.VMEM((TT, D_MODEL), F32),
            pltpu.VMEM((N_LANE_TILES, TT, LANES), F32),
            pltpu.VMEM((N_HEADS, HEAD_DIM, HEAD_DIM), F32),
            pltpu.VMEM((N_HEADS, 1, HEAD_DIM), F32),
            pltpu.VMEM((N_HEADS, SUBLANES, LANES), F32),
        ],
        compiler_params=pltpu.CompilerParams(
            dimension_semantics=("arbitrary", "arbitrary"),
            vmem_limit_bytes=VMEM_LIMIT_BYTES),
        name="mixer",
    )(*args)


def _mlp_kernel(x_ref, mod_ref, gpre_ref, gpost_ref, w1_ref, w2_ref, o_ref):
    sh = mod_ref[0]
    sc = mod_ref[1]
    gt = mod_ref[2]
    gs = gpre_ref[...] * (1.0 + sc)
    for half in range(TM // TM_HALF):
        rows = slice(half * TM_HALF, (half + 1) * TM_HALF)
        x = x_ref[0, rows, :]
        hb = (x * _rms_scale(x) * gs + sh).astype(BF16)
        f = jnp.zeros((TM_HALF, D_MODEL), F32)
        for c in range(D_FF // D_MODEL):
            cs = slice(c * D_MODEL, (c + 1) * D_MODEL)
            u = jnp.maximum(_dot(hb, w1_ref[:, cs]), 0.0)
            f = f + _dot((u * u).astype(BF16), w2_ref[cs, :])
        o_ref[0, rows, :] = x + gt * (f * _rms_scale(f) * gpost_ref[...])


def _mlp_call(layer, x, mod, p):
    batch, seq, _ = x.shape
    n_t = seq // TM
    row = lambda b, t: (layer, 0, 0)
    return pl.pallas_call(
        _mlp_kernel,
        out_shape=jax.ShapeDtypeStruct(x.shape, F32),
        grid=(batch, n_t),
        in_specs=[
            pl.BlockSpec((1, TM, D_MODEL), lambda b, t: (b, t, 0)),
            pl.BlockSpec((None, 3, None, 1, D_MODEL), lambda b, t: (layer, 0, b, 0, 0)),
            pl.BlockSpec((None, 1, D_MODEL), row, pipeline_mode=pl.Buffered(1)),
            pl.BlockSpec((None, 1, D_MODEL), row, pipeline_mode=pl.Buffered(1)),
            pl.BlockSpec((None, D_MODEL, D_FF), row, pipeline_mode=pl.Buffered(1)),
            pl.BlockSpec((None, D_FF, D_MODEL), row, pipeline_mode=pl.Buffered(1)),
        ],
        out_specs=pl.BlockSpec((1, TM, D_MODEL), lambda b, t: (b, t, 0)),
        compiler_params=pltpu.CompilerParams(
            dimension_semantics=("arbitrary", "arbitrary"),
            vmem_limit_bytes=VMEM_LIMIT_BYTES),
        name="mlp",
    )(x, mod, p["g_pre_mlp"], p["g_post_mlp"], p["w_ff1"], p["w_ff2"])


def kernel(x, c, w_ada, b_ada, g_pre_mix, g_post_mix, g_pre_mlp, g_post_mlp, w_in, w_conv_m, b_conv_m, w_q, w_k, w_v, w_gates, b_gates, g_mh, w_dw, b_dw, g_cn, b_cn, w_out, w_ff1, w_ff2):
    depth = w_in.shape[0]
    batch = x.shape[0]
    assert x.shape[1] % TM == 0 and x.shape[1] % TT == 0 and TT % CHUNK == 0

    def rows(a):
        return a.reshape(depth, 1, a.shape[-1]).astype(F32)

    wg = w_gates.reshape(depth, 3, HEAD_DIM, N_HEADS, 2 * N_HEADS)
    wg = wg.transpose(0, 1, 3, 2, 4).reshape(depth, 3, D_MODEL, 2 * N_HEADS)
    wg = jnp.pad(wg, ((0, 0), (0, 0), (0, 0), (0, LANES - 2 * N_HEADS))).astype(BF16)
    bg = jnp.pad(b_gates.astype(F32), ((0, 0), (0, LANES - 2 * N_HEADS))).reshape(depth, 1, LANES)

    p = {
        "g_pre_mix": rows(g_pre_mix), "g_post_mix": rows(g_post_mix),
        "g_pre_mlp": rows(g_pre_mlp), "g_post_mlp": rows(g_post_mlp),
        "w_in": w_in.astype(BF16), "w_conv_m": w_conv_m.astype(F32), "b_conv_m": rows(b_conv_m),
        "w_q": w_q.astype(BF16), "w_k": w_k.astype(BF16), "w_v": w_v.astype(BF16),
        "w_gates": wg, "b_gates": bg, "g_mh": rows(g_mh),
        "w_dw": w_dw.astype(F32), "b_dw": rows(b_dw), "g_cn": rows(g_cn), "b_cn": rows(b_cn),
        "w_out": w_out.astype(BF16), "w_ff1": w_ff1.astype(BF16), "w_ff2": w_ff2.astype(BF16),
    }

    mod = _ada_call(c.astype(F32), w_ada.astype(F32), b_ada.astype(F32))
    mod = mod.reshape(depth, 2, 3, batch, 1, D_MODEL)

    for layer in range(depth):
        x = _mixer_call(layer, x, mod[:, 0], p)
        x = _mlp_call(layer, x, mod[:, 1], p)
    return x
```

```python
import functools

import jax
import jax.numpy as jnp
from jax import lax
from jax.experimental import pallas as pl
from jax.experimental.pallas import tpu as pltpu

D_MODEL = 1024
N_HEADS = 4
HEAD_DIM = 256
N_CONV_GROUPS = 8
CONV_GROUP = D_MODEL // N_CONV_GROUPS
K_M = 4
K_C = 31
CHUNK = 512
D_FF = 4 * D_MODEL
N_ADA = 6
EPS = 1e-6

SUBLANES = 8
LANES = 128
VMEM_LIMIT_BYTES = 56 * 1024 * 1024

TT = 512
TM = 512
TM_PART = TM
HIST_M = SUBLANES
HIST_C = 4 * SUBLANES
CONV_ROWS = 64
N_LANE_TILES = D_MODEL // LANES
assert CONV_GROUP == LANES

BF16 = jnp.bfloat16
F32 = jnp.float32


def _sigmoid(x):
    return jax.nn.sigmoid(x)


def _log_sigmoid(x):
    return jnp.minimum(x, 0.0) - jnp.log1p(jnp.exp(-jnp.abs(x)))


def _rms_scale(x):
    return lax.rsqrt(jnp.mean(x * x, axis=-1, keepdims=True) + EPS)


def _dot(a, b):
    return jnp.dot(a, b, preferred_element_type=F32)


def _ada_kernel(c_ref, w_ref, b_ref, o_ref):
    c = c_ref[...]
    c_act = c * _sigmoid(c)
    o_ref[0, 0] = _dot(c_act, w_ref[0]) + b_ref[0, 0]


def _ada_call(c, w_ada, b_ada):
    depth = w_ada.shape[0]
    batch = c.shape[0]
    return pl.pallas_call(
        _ada_kernel,
        out_shape=jax.ShapeDtypeStruct((depth, N_ADA, batch, D_MODEL), F32),
        grid=(depth, N_ADA),
        in_specs=[
            pl.BlockSpec((batch, D_MODEL), lambda l, j: (0, 0)),
            pl.BlockSpec((1, D_MODEL, D_MODEL), lambda l, j: (l, 0, j)),
            pl.BlockSpec((1, 1, 1, D_MODEL), lambda l, j: (l, j, 0, 0)),
        ],
        out_specs=pl.BlockSpec((1, 1, batch, D_MODEL), lambda l, j: (l, j, 0, 0)),
        compiler_params=pltpu.CompilerParams(
            dimension_semantics=("arbitrary", "arbitrary"),
            vmem_limit_bytes=VMEM_LIMIT_BYTES),
        name="ada_mod",
    )(c, w_ada, b_ada.reshape(depth, N_ADA, 1, D_MODEL))


def _causal_dwconv(src_ref, hist, w_ref, b_ref, k, lt, post, dst_ref):
    cols = slice(lt * LANES, (lt + 1) * LANES)
    starts = [span * 2 * CONV_ROWS + phase for span in range(TT // (2 * CONV_ROWS)) for phase in range(2)]
    bias = jnp.broadcast_to(b_ref[:, cols], (CONV_ROWS, LANES))
    accs = [bias for _ in starts]
    for j in range(k):
        w = w_ref[j:j + 1, cols]
        for n, r0 in enumerate(starts):
            off = r0 + hist - (k - 1) + j
            accs[n] = accs[n] + w * src_ref[lt, pl.ds(off, CONV_ROWS, stride=2), :]
    for n, r0 in enumerate(starts):
        dst_ref[lt, pl.ds(r0, CONV_ROWS, stride=2), :] = post(accs[n], cols)


def _mixer_kernel(x_ref, mod_ref, gpre_ref, gpost_ref, win_ref, wcm_ref, bcm_ref,
                  wq_ref, wk_ref, wv_ref, wg_ref, bg_ref, gmh_ref, wdw_ref, bdw_ref,
                  gcn_ref, bcn_ref, wout_ref, o_ref,
                  xm_ref, a_ref, xc_ref, q_ref, k_ref, v_ref, z_ref, hm_ref, ac_ref,
                  c_state, n_state, m_state):
    t_idx = pl.program_id(1)

    @pl.when(t_idx == 0)
    def _():
        xm_ref[:, 0:HIST_M, :] = jnp.zeros((N_LANE_TILES, HIST_M, LANES), F32)
        a_ref[:, 0:HIST_C, :] = jnp.zeros((N_LANE_TILES, HIST_C, LANES), F32)
        c_state[...] = jnp.zeros(c_state.shape, F32)
        n_state[...] = jnp.zeros(n_state.shape, F32)
        m_state[...] = jnp.zeros(m_state.shape, F32)

    x = x_ref[0]
    sh = mod_ref[0]
    sc = mod_ref[1]
    gt = mod_ref[2]

    h = x * _rms_scale(x) * (gpre_ref[...] * (1.0 + sc)) + sh
    hb = h.astype(BF16)

    for hd in range(N_HEADS):
        ca = slice(2 * D_MODEL + hd * HEAD_DIM, 2 * D_MODEL + (hd + 1) * HEAD_DIM)
        cb = slice(3 * D_MODEL + hd * HEAD_DIM, 3 * D_MODEL + (hd + 1) * HEAD_DIM)
        a_val = _dot(hb, win_ref[:, ca]) * _sigmoid(_dot(hb, win_ref[:, cb]))
        for half in range(HEAD_DIM // LANES):
            a_ref[2 * hd + half, HIST_C:HIST_C + TT, :] = a_val[:, half * LANES:(half + 1) * LANES]
    x_m = _dot(hb, win_ref[:, 0:D_MODEL])
    for lt in range(N_LANE_TILES):
        xm_ref[lt, HIST_M:HIST_M + TT, :] = x_m[:, lt * LANES:(lt + 1) * LANES]
    z_ref[...] = _dot(hb, win_ref[:, D_MODEL:2 * D_MODEL])

    def swish(acc, cols):
        return acc * _sigmoid(acc)

    for lt in range(N_LANE_TILES):
        _causal_dwconv(xm_ref, HIST_M, wcm_ref, bcm_ref, K_M, lt, swish, xc_ref)
    xmb = x_m.astype(BF16)
    for hd in range(N_HEADS):
        sl = slice(hd * HEAD_DIM, (hd + 1) * HEAD_DIM)
        xcb = jnp.concatenate([xc_ref[2 * hd], xc_ref[2 * hd + 1]], axis=1).astype(BF16)
        q_ref[:, sl] = _dot(xcb, wq_ref[hd])
        k_ref[:, sl] = _dot(xcb, wk_ref[hd])
        v_ref[:, sl] = _dot(xmb[:, sl], wv_ref[hd])

    gates = (_dot(q_ref[...].astype(BF16), wg_ref[0])
             + _dot(k_ref[...].astype(BF16), wg_ref[1])
             + _dot(v_ref[...].astype(BF16), wg_ref[2]) + bg_ref[...])
    gates_t = gates.T
    g8 = gates_t[0:SUBLANES, :]
    lf8 = _log_sigmoid(g8)

    lane = lax.broadcasted_iota(jnp.int32, (SUBLANES, TT), 1)
    lane_in_chunk = lane % CHUNK
    cs8 = lf8
    shift = 1
    while shift < CHUNK:
        rolled = pltpu.roll(cs8, shift, axis=1)
        cs8 = cs8 + jnp.where(lane_in_chunk >= shift, rolled, 0.0)
        shift *= 2

    row_i = lax.broadcasted_iota(jnp.int32, (CHUNK, CHUNK), 0)
    col_i = lax.broadcasted_iota(jnp.int32, (CHUNK, CHUNK), 1)
    causal = col_i <= row_i
    diag = col_i == row_i

    for ck in range(TT // CHUNK):
        rows = slice(ck * CHUNK, (ck + 1) * CHUNK)
        for hd in range(N_HEADS):
            sl = slice(hd * HEAD_DIM, (hd + 1) * HEAD_DIM)
            ig_row = g8[hd:hd + 1, rows]
            lf_row = lf8[N_HEADS + hd:N_HEADS + hd + 1, rows]
            b_row = cs8[N_HEADS + hd:N_HEADS + hd + 1, rows]
            b_col = jnp.sum(jnp.where(causal, lf_row, 0.0), axis=1, keepdims=True)
            ig_col = jnp.sum(jnp.where(diag, ig_row, 0.0), axis=1, keepdims=True)
            b_last = jnp.sum(lf_row, axis=1, keepdims=True)
            m_prev = m_state[hd, 0:1, 0:1]

            logw = jnp.where(causal, b_col + (ig_row - b_row), -jnp.inf)
            g_col = b_col + m_prev
            m_t = jnp.maximum(g_col, jnp.max(logw, axis=1, keepdims=True))
            w = jnp.exp(logw - m_t)
            inter = jnp.exp(g_col - m_t)

            qf = q_ref[rows, sl]
            kf = k_ref[rows, sl] * (HEAD_DIM ** -0.5)
            vb = v_ref[rows, sl].astype(BF16)
            qb = qf.astype(BF16)
            s = lax.dot_general(qb, kf.astype(BF16), (((1,), (1,)), ((), ())),
                                preferred_element_type=F32) * w
            c_prev = c_state[hd]
            n_prev = n_state[hd]
            num = _dot(s.astype(BF16), vb) + inter * _dot(qb, c_prev.astype(BF16))
            den = (jnp.sum(s, axis=1, keepdims=True)
                   + inter * jnp.sum(qf * n_prev, axis=1, keepdims=True))
            hm_ref[rows, sl] = num / jnp.maximum(jnp.abs(den), jnp.exp(-m_t))

            a_col = b_last - b_col + ig_col
            m_new = jnp.maximum(b_last + m_prev, jnp.max(a_col, axis=0, keepdims=True))
            wk = jnp.exp(a_col - m_new)
            decay = jnp.exp(b_last + m_prev - m_new)
            kw = kf * wk
            c_state[hd] = decay * c_prev + lax.dot_general(
                kw.astype(BF16), vb, (((0,), (0,)), ((), ())), preferred_element_type=F32)
            n_state[hd] = decay * n_prev + jnp.sum(kw, axis=0, keepdims=True)
            m_state[hd] = jnp.broadcast_to(m_new, (SUBLANES, LANES))

    for hd in range(N_HEADS):
        sl = slice(hd * HEAD_DIM, (hd + 1) * HEAD_DIM)
        hh = hm_ref[:, sl]
        mu = jnp.mean(hh, axis=-1, keepdims=True)
        hc = hh - mu
        var = jnp.mean(hc * hc, axis=-1, keepdims=True)
        hn = hc * lax.rsqrt(var + EPS) * gmh_ref[:, sl]
        hm_ref[:, sl] = _sigmoid(z_ref[:, sl]) * hn

    def ln_swish(acc, cols):
        mu = jnp.mean(acc, axis=-1, keepdims=True)
        ctr = acc - mu
        var = jnp.mean(ctr * ctr, axis=-1, keepdims=True)
        y = ctr * lax.rsqrt(var + EPS) * gcn_ref[:, cols] + bcn_ref[:, cols]
        return y * _sigmoid(y)

    for lt in range(N_LANE_TILES):
        _causal_dwconv(a_ref, HIST_C, wdw_ref, bdw_ref, K_C, lt, ln_swish, ac_ref)

    ac = jnp.concatenate([ac_ref[lt] for lt in range(N_LANE_TILES)], axis=1)
    y = (_dot(hm_ref[...].astype(BF16), wout_ref[0:D_MODEL, :])
         + _dot(ac.astype(BF16), wout_ref[D_MODEL:2 * D_MODEL, :]))
    o_ref[0] = x + gt * (y * _rms_scale(y) * gpost_ref[...])

    xm_ref[:, 0:HIST_M, :] = xm_ref[:, TT:HIST_M + TT, :]
    a_ref[:, 0:HIST_C, :] = a_ref[:, TT:HIST_C + TT, :]


def _mixer_call(layer, x, mod, p):
    batch, seq, _ = x.shape
    n_t = seq // TT

    single = pl.Buffered(1)

    def row(name):
        return pl.BlockSpec((None, 1, D_MODEL), lambda b, t: (layer, 0, 0), pipeline_mode=single), p[name]

    def full(name):
        arr = p[name]
        blk = (None,) + arr.shape[1:]
        nd = arr.ndim - 1
        return pl.BlockSpec(blk, lambda b, t: (layer,) + (0,) * nd, pipeline_mode=single), arr

    specs_args = [
        (pl.BlockSpec((1, TT, D_MODEL), lambda b, t: (b, t, 0)), x),
        (pl.BlockSpec((None, 3, None, 1, D_MODEL), lambda b, t: (layer, 0, b, 0, 0)), mod),
        row("g_pre_mix"), row("g_post_mix"), full("w_in"), full("w_conv_m"), row("b_conv_m"),
        full("w_q"), full("w_k"), full("w_v"), full("w_gates"),
        (pl.BlockSpec((None, 1, LANES), lambda b, t: (layer, 0, 0)), p["b_gates"]),
        row("g_mh"), full("w_dw"), row("b_dw"), row("g_cn"), row("b_cn"), full("w_out"),
    ]
    in_specs = [s for s, _ in specs_args]
    args = [a for _, a in specs_args]
    return pl.pallas_call(
        _mixer_kernel,
        out_shape=jax.ShapeDtypeStruct(x.shape, F32),
        grid=(batch, n_t),
        in_specs=in_specs,
        out_specs=pl.BlockSpec((1, TT, D_MODEL), lambda b, t: (b, t, 0)),
        scratch_shapes=[
            pltpu.VMEM((N_LANE_TILES, HIST_M + TT, LANES), F32),
            pltpu.VMEM((N_LANE_TILES, HIST_C + TT, LANES), F32),
            pltpu.VMEM((N_LANE_TILES, TT, LANES), F32),
            pltpu.VMEM((TT, D_MODEL), F32),
            pltpu.VMEM((TT, D_MODEL), F32),
            pltpu.VMEM((TT, D_MODEL), F32),
            pltpu.VMEM((TT, D_MODEL), F32),
            pltpu.VMEM((TT, D_MODEL), F32),
            pltpu.VMEM((N_LANE_TILES, TT, LANES), F32),
            pltpu.VMEM((N_HEADS, HEAD_DIM, HEAD_DIM), F32),
            pltpu.VMEM((N_HEADS, 1, HEAD_DIM), F32),
            pltpu.VMEM((N_HEADS, SUBLANES, LANES), F32),
        ],
        compiler_params=pltpu.CompilerParams(
            dimension_semantics=("arbitrary", "arbitrary"),
            vmem_limit_bytes=VMEM_LIMIT_BYTES),
        name="mixer",
    )(*args)


def _mlp_kernel(x_ref, mod_ref, gpre_ref, gpost_ref, w1_ref, w2_ref, o_ref):
    sh = mod_ref[0]
    sc = mod_ref[1]
    gt = mod_ref[2]
    gs = gpre_ref[...] * (1.0 + sc)
    for part in range(TM // TM_PART):
        rows = slice(part * TM_PART, (part + 1) * TM_PART)
        x = x_ref[0, rows, :]
        h = x * _rms_scale(x) * gs + sh
        f = jnp.zeros((TM_PART, D_MODEL), F32)
        for c in range(D_FF // D_MODEL):
            cs = slice(c * D_MODEL, (c + 1) * D_MODEL)
            u = jnp.maximum(_dot(h, w1_ref[:, cs]), 0.0)
            f = f + _dot(u * u, w2_ref[cs, :])
        o_ref[0, rows, :] = x + gt * (f * _rms_scale(f) * gpost_ref[...])


def _mlp_call(layer, x, mod, p):
    batch, seq, _ = x.shape
    n_t = seq // TM
    row = lambda b, t: (layer, 0, 0)
    return pl.pallas_call(
        _mlp_kernel,
        out_shape=jax.ShapeDtypeStruct(x.shape, F32),
        grid=(batch, n_t),
        in_specs=[
            pl.BlockSpec((1, TM, D_MODEL), lambda b, t: (b, t, 0)),
            pl.BlockSpec((None, 3, None, 1, D_MODEL), lambda b, t: (layer, 0, b, 0, 0)),
            pl.BlockSpec((None, 1, D_MODEL), row, pipeline_mode=pl.Buffered(1)),
            pl.BlockSpec((None, 1, D_MODEL), row, pipeline_mode=pl.Buffered(1)),
            pl.BlockSpec((None, D_MODEL, D_FF), row, pipeline_mode=pl.Buffered(1)),
            pl.BlockSpec((None, D_FF, D_MODEL), row, pipeline_mode=pl.Buffered(1)),
        ],
        out_specs=pl.BlockSpec((1, TM, D_MODEL), lambda b, t: (b, t, 0)),
        compiler_params=pltpu.CompilerParams(
            dimension_semantics=("arbitrary", "arbitrary"),
            vmem_limit_bytes=VMEM_LIMIT_BYTES),
        name="mlp",
    )(x, mod, p["g_pre_mlp"], p["g_post_mlp"], p["w_ff1"], p["w_ff2"])


def kernel(x, c, w_ada, b_ada, g_pre_mix, g_post_mix, g_pre_mlp, g_post_mlp, w_in, w_conv_m, b_conv_m, w_q, w_k, w_v, w_gates, b_gates, g_mh, w_dw, b_dw, g_cn, b_cn, w_out, w_ff1, w_ff2):
    depth = w_in.shape[0]
    batch = x.shape[0]
    assert x.shape[1] % TM == 0 and x.shape[1] % TT == 0 and TT % CHUNK == 0

    def rows(a):
        return a.reshape(depth, 1, a.shape[-1]).astype(F32)

    wg = w_gates.reshape(depth, 3, HEAD_DIM, N_HEADS, 2 * N_HEADS)
    wg = wg.transpose(0, 1, 3, 2, 4).reshape(depth, 3, D_MODEL, 2 * N_HEADS)
    wg = jnp.pad(wg, ((0, 0), (0, 0), (0, 0), (0, LANES - 2 * N_HEADS))).astype(BF16)
    bg = jnp.pad(b_gates.astype(F32), ((0, 0), (0, LANES - 2 * N_HEADS))).reshape(depth, 1, LANES)

    p = {
        "g_pre_mix": rows(g_pre_mix), "g_post_mix": rows(g_post_mix),
        "g_pre_mlp": rows(g_pre_mlp), "g_post_mlp": rows(g_post_mlp),
        "w_in": w_in.astype(BF16), "w_conv_m": w_conv_m.astype(F32), "b_conv_m": rows(b_conv_m),
        "w_q": w_q.astype(BF16), "w_k": w_k.astype(BF16), "w_v": w_v.astype(BF16),
        "w_gates": wg, "b_gates": bg, "g_mh": rows(g_mh),
        "w_dw": w_dw.astype(F32), "b_dw": rows(b_dw), "g_cn": rows(g_cn), "b_cn": rows(b_cn),
        "w_out": w_out.astype(BF16), "w_ff1": w_ff1.astype(F32), "w_ff2": w_ff2.astype(F32),
    }

    mod = _ada_call(c.astype(F32), w_ada.astype(F32), b_ada.astype(F32))
    mod = mod.reshape(depth, 2, 3, batch, 1, D_MODEL)

    for layer in range(depth):
        x = _mixer_call(layer, x, mod[:, 0], p)
        x = _mlp_call(layer, x, mod[:, 1], p)
    return x
```
